```python
import math
import jax, jax.numpy as jnp
from jax import lax
import numpy as np

D_MODEL = 4096
BATCH = 32
SEQ = 256
DEPTH = 4
DEC_BATCH = 8
DEC_SEQ = 4096
PAST_LEN = 512

GRID_W = 64
N_MIXERS = 3
N_HYENA = (DEPTH + 2) // 3
N_SCONV = (DEPTH + 1) // 3
N_GLA = DEPTH // 3
WIDTH = D_MODEL
DN_ALPHA = (2 * DEPTH) ** 0.25
DN_BETA = (8 * DEPTH) ** -0.25
LN_EPS = 1e-5
HY_BANDS = 16
HY_EMB = 1 + 2 * HY_BANDS
HY_HIDDEN = 64
HY_N_INNER = 2
HY_TARGET = 1e-2
HY_FAST_PCT = 0.3
HY_SLOW_PCT = 1.5
HY_MIN_DECAY = math.log(HY_TARGET) / HY_SLOW_PCT
HY_MAX_DECAY = math.log(HY_TARGET) / HY_FAST_PCT
GLA_HEADS = 4
GLA_DK = D_MODEL // 2 // GLA_HEADS
GLA_DV = D_MODEL // GLA_HEADS
GLA_RANK = 16
GLA_TAU = 16.0
GLA_CHUNK = 32
GLA_NORM_EPS = 1e-6

kernel_name = 'hybrid_diffusion_hyena_sconv_gla_step'


def _layer_norm(x, g, b):
    xf = x.astype(jnp.float32)
    mu = jnp.mean(xf, axis=-1, keepdims=True)
    var = jnp.mean(jnp.square(xf - mu), axis=-1, keepdims=True)
    return ((xf - mu) * lax.rsqrt(var + LN_EPS)).astype(x.dtype) * g + b


def _modulation(cond, w, b):
    m = jax.nn.silu(cond) @ w + b
    return jnp.split(m, 3, axis=-1)


def _conv3(x, w, grid):
    bsz, length, ch = x.shape
    if grid:
        rows = length // GRID_W
        x = x.reshape(bsz * rows, GRID_W, ch)
    xp = jnp.pad(x, ((0, 0), (1, 1), (0, 0)))
    y = xp[:, :-2] * w[0] + xp[:, 1:-1] * w[1] + xp[:, 2:] * w[2]
    return y.reshape(bsz, length, ch)


def _hyena_filters(length, fw_in, fb_in, fw_mid, fb_mid, freq, fw_out, fb_out):
    t = jnp.arange(length, dtype=jnp.float32) / length
    bands = jnp.linspace(1e-4, HY_BANDS - 1, HY_BANDS, dtype=jnp.float32)
    ang = (2.0 * math.pi) * t[:, None] * bands[None, :]
    z = jnp.concatenate([t[:, None], jnp.cos(ang), -jnp.sin(ang)], axis=-1).astype(fw_in.dtype)
    h = jnp.sin(freq * (z @ fw_in + fb_in))
    for m in range(HY_N_INNER):
        h = jnp.sin(freq * (h @ fw_mid[m] + fb_mid[m]))
    h = (h @ fw_out + fb_out).astype(jnp.float32).reshape(length, 2, WIDTH)
    deltas = jnp.abs(jnp.linspace(HY_MIN_DECAY, HY_MAX_DECAY, WIDTH, dtype=jnp.float32))
    window = jnp.exp(-t[:, None] * deltas[None, :])
    h = h * window[:, None, :]
    return h[:, 0], h[:, 1]


def _bidir_long_conv(u, h_fwd, h_bwd):
    length = u.shape[1]
    n_fft = 2 * length
    k = jnp.concatenate([h_fwd, jnp.zeros((1, WIDTH), jnp.float32), h_bwd[1:][::-1]], axis=0)
    k_f = jnp.fft.rfft(k, n=n_fft, axis=0)
    u_f = jnp.fft.rfft(u.astype(jnp.float32), n=n_fft, axis=1)
    y = jnp.fft.irfft(u_f * k_f[None], n=n_fft, axis=1)[:, :length]
    return y.astype(u.dtype)


def _hyena_mixer(u, grid, w_in, conv_w, conv_b, fw_in, fb_in, fw_mid, fb_mid, freq,
                 fw_out, fb_out, skip, w_out):
    length = u.shape[1]
    proj = u @ w_in
    streams = _conv3(proj[..., :3 * WIDTH], conv_w, grid) + conv_b
    x0, x1, v = jnp.split(streams, 3, axis=-1)
    gate = jax.nn.silu(proj[..., 3 * WIDTH:])
    h_fwd, h_bwd = _hyena_filters(length, fw_in, fb_in, fw_mid, fb_mid, freq, fw_out, fb_out)
    v = v * x1
    v = _bidir_long_conv(v, h_fwd, h_bwd) + v * skip
    return (x0 * v * gate) @ w_out


def _short_conv_mixer(u, grid, w_in, conv_w, w_out):
    proj = u @ w_in
    b_gate, c_gate, xv, g = jnp.split(proj, 4, axis=-1)
    y = b_gate * _conv3(c_gate * xv, conv_w, grid)
    return (y * jax.nn.silu(g)) @ w_out


def _gla_project(u, w_in, wg1, wg2, bg):
    bsz, length, _ = u.shape
    dk_tot = GLA_HEADS * GLA_DK
    dv_tot = GLA_HEADS * GLA_DV
    proj = u @ w_in
    q, k, v, r = jnp.split(proj, [dk_tot, 2 * dk_tot, 2 * dk_tot + dv_tot], axis=-1)
    heads = lambda a, d: a.reshape(bsz, length, GLA_HEADS, d).transpose(0, 2, 1, 3)
    q = heads(q, GLA_DK) * (GLA_DK ** -0.5)
    k = heads(k, GLA_DK)
    v = heads(v, GLA_DV)
    low = jnp.einsum('bld,zdr->zblr', u, wg1)
    logits = jnp.einsum('zblr,zrk->zblk', low, wg2)
    g = jax.nn.log_sigmoid(logits.astype(jnp.float32) + bg[:, None, None, :].astype(jnp.float32)) / GLA_TAU
    return q, k, v, r, heads(g[0], GLA_DK), heads(g[1], GLA_DK)


def _gla_chunk_scan(q, k, v, g, s0):
    bsz, nh, length, _ = q.shape
    n_chunks = length // GLA_CHUNK
    def to_chunks(a):
        return jnp.moveaxis(a.astype(jnp.float32).reshape(bsz, nh, n_chunks, GLA_CHUNK, a.shape[-1]), 2, 0)
    tri = jnp.tril(jnp.ones((GLA_CHUNK, GLA_CHUNK), dtype=bool))[:, :, None]
    def step(state, inp):
        qc, kc, vc, gc = inp
        b = jnp.cumsum(gc, axis=2)
        diff = b[:, :, :, None, :] - b[:, :, None, :, :]
        decay = jnp.where(tri, jnp.exp(jnp.where(tri, diff, 0.0)), 0.0)
        attn = jnp.einsum('bhtd,bhsd,bhtsd->bhts', qc, kc, decay)
        intra = jnp.einsum('bhts,bhse->bhte', attn, vc)
        inter = jnp.einsum('bhtd,bhde->bhte', qc * jnp.exp(b), state)
        b_last = b[:, :, -1:, :]
        new_state = jnp.exp(b[:, :, -1, :])[..., None] * state + jnp.einsum(
            'bhsd,bhse->bhde', kc * jnp.exp(b_last - b), vc)
        return new_state, inter + intra
    s_fin, out = lax.scan(step, s0.astype(jnp.float32), (to_chunks(q), to_chunks(k), to_chunks(v), to_chunks(g)))
    out = jnp.moveaxis(out, 0, 2).reshape(bsz, nh, length, GLA_DV)
    return out, s_fin


def _flip(a):
    return a[:, :, ::-1]


def _gla_output(o, r, norm_g, w_out):
    bsz, nh, length, _ = o.shape
    o = o * lax.rsqrt(jnp.mean(o * o, axis=-1, keepdims=True) + GLA_NORM_EPS)
    o = o.transpose(0, 2, 1, 3).astype(r.dtype) * norm_g
    o = o.reshape(bsz, length, nh * GLA_DV)
    return (o * jax.nn.silu(r)) @ w_out


def _gla_context(u, w_in, wg1, wg2, bg, norm_g, w_out):
    q, k, v, r, g_f, g_b = _gla_project(u, w_in, wg1, wg2, bg)
    zero = jnp.zeros((u.shape[0], GLA_HEADS, GLA_DK, GLA_DV), jnp.float32)
    o_f, s_f = _gla_chunk_scan(q, k, v, g_f, zero)
    o_b, s_b = _gla_chunk_scan(_flip(q), _flip(k), _flip(v), _flip(g_b), zero)
    y = _gla_output(o_f + _flip(o_b), r, norm_g, w_out)
    return y, jnp.stack([s_f, s_b], axis=1).astype(u.dtype)


def _gla_latent(u, state, w_in, wg1, wg2, bg, norm_g, w_out):
    q, k, v, r, g_f, g_b = _gla_project(u, w_in, wg1, wg2, bg)
    o_f, _ = _gla_chunk_scan(q, k, v, g_f, state[:, 0])
    o_b, _ = _gla_chunk_scan(_flip(q), _flip(k), _flip(v), _flip(g_b), state[:, 1])
    return _gla_output(o_f + _flip(o_b), r, norm_g, w_out)


def setup_inputs(seed: int = 0) -> dict:
    key = jax.random.key(seed)
    ks = jax.random.split(key, 48)
    counter = [0]
    def nrm(shape, s):
        kk = ks[counter[0]]
        counter[0] += 1
        return jax.random.normal(kk, shape, jnp.float32) * s
    W = WIDTH
    dk_tot = GLA_HEADS * GLA_DK
    dv_tot = GLA_HEADS * GLA_DV
    inputs = {}
    inputs['x_prompt'] = nrm((BATCH, SEQ, D_MODEL), 1.0)
    inputs['x_sample'] = nrm((DEC_BATCH, DEC_SEQ, D_MODEL), 1.0)
    inputs['state_gla'] = nrm((DEC_BATCH, N_GLA, 2, GLA_HEADS, GLA_DK, GLA_DV), 1.0)
    inputs['c'] = nrm((DEC_BATCH, D_MODEL), 1.0)
    inputs['c_ctx'] = nrm((D_MODEL,), 1.0)
    inputs['ada_w'] = nrm((DEPTH, D_MODEL, 3 * D_MODEL), D_MODEL ** -0.5)
    inputs['ada_b'] = nrm((DEPTH, 3 * D_MODEL), 0.01)
    inputs['ln_g'] = 1.0 + nrm((DEPTH, D_MODEL), 0.02)
    inputs['ln_b'] = nrm((DEPTH, D_MODEL), 0.01)
    inputs['hy_w_in'] = nrm((N_HYENA, D_MODEL, 4 * W), D_MODEL ** -0.5)
    inputs['hy_conv_w'] = nrm((N_HYENA, 3, 3 * W), 0.5)
    inputs['hy_conv_b'] = nrm((N_HYENA, 3 * W), 0.01)
    inputs['hy_fw_in'] = nrm((N_HYENA, HY_EMB, HY_HIDDEN), HY_EMB ** -0.5)
    inputs['hy_fb_in'] = nrm((N_HYENA, HY_HIDDEN), 0.1)
    inputs['hy_fw_mid'] = nrm((N_HYENA, HY_N_INNER, HY_HIDDEN, HY_HIDDEN), HY_HIDDEN ** -0.5)
    inputs['hy_fb_mid'] = nrm((N_HYENA, HY_N_INNER, HY_HIDDEN), 0.1)
    inputs['hy_freq'] = 1.0 + nrm((N_HYENA, HY_HIDDEN), 0.01)
    inputs['hy_fw_out'] = nrm((N_HYENA, HY_HIDDEN, 2 * W), 0.02)
    inputs['hy_fb_out'] = nrm((N_HYENA, 2 * W), 0.01)
    inputs['hy_skip'] = nrm((N_HYENA, W), 1.0)
    inputs['hy_w_out'] = nrm((N_HYENA, W, D_MODEL), (W ** -0.5) * DN_BETA)
    inputs['sc_w_in'] = nrm((N_SCONV, D_MODEL, 4 * W), D_MODEL ** -0.5)
    inputs['sc_conv_w'] = nrm((N_SCONV, 3, W), 0.5)
    inputs['sc_w_out'] = nrm((N_SCONV, W, D_MODEL), (W ** -0.5) * DN_BETA)
    inputs['gla_w_in'] = nrm((N_GLA, D_MODEL, 2 * dk_tot + 2 * dv_tot), D_MODEL ** -0.5)
    inputs['gla_wg1'] = nrm((N_GLA, 2, D_MODEL, GLA_RANK), D_MODEL ** -0.5)
    inputs['gla_wg2'] = nrm((N_GLA, 2, GLA_RANK, dk_tot), GLA_RANK ** -0.5)
    inputs['gla_bg'] = nrm((N_GLA, 2, dk_tot), 0.01)
    inputs['gla_norm_g'] = 1.0 + nrm((N_GLA, GLA_DV), 0.02)
    inputs['gla_w_out'] = nrm((N_GLA, dv_tot, D_MODEL), (dv_tot ** -0.5) * DN_BETA)
    return inputs


def reference(x_prompt, x_sample, state_gla, c, c_ctx, ada_w, ada_b, ln_g, ln_b,
              hy_w_in, hy_conv_w, hy_conv_b, hy_fw_in, hy_fb_in, hy_fw_mid, hy_fb_mid,
              hy_freq, hy_fw_out, hy_fb_out, hy_skip, hy_w_out,
              sc_w_in, sc_conv_w, sc_w_out,
              gla_w_in, gla_wg1, gla_wg2, gla_bg, gla_norm_g, gla_w_out):
    yp = x_prompt
    ys = x_sample
    new_states = []
    for i in range(DEPTH):
        kind = i % N_MIXERS
        j = i // N_MIXERS
        sh_p, sc_p, gt_p = _modulation(c_ctx, ada_w[i], ada_b[i])
        sh_s, sc_s, gt_s = [m[:, None, :] for m in _modulation(c, ada_w[i], ada_b[i])]
        up = yp * (1.0 + sc_p) + sh_p
        us = ys * (1.0 + sc_s) + sh_s
        if kind == 0:
            hy_args = (hy_w_in[j], hy_conv_w[j], hy_conv_b[j], hy_fw_in[j], hy_fb_in[j],
                       hy_fw_mid[j], hy_fb_mid[j], hy_freq[j], hy_fw_out[j], hy_fb_out[j],
                       hy_skip[j], hy_w_out[j])
            hp = _hyena_mixer(up, False, *hy_args)
            hs = _hyena_mixer(us, True, *hy_args)
        elif kind == 1:
            hp = _short_conv_mixer(up, False, sc_w_in[j], sc_conv_w[j], sc_w_out[j])
            hs = _short_conv_mixer(us, True, sc_w_in[j], sc_conv_w[j], sc_w_out[j])
        else:
            gla_args = (gla_w_in[j], gla_wg1[j], gla_wg2[j], gla_bg[j], gla_norm_g[j], gla_w_out[j])
            hp, st = _gla_context(up, *gla_args)
            hs = _gla_latent(us, state_gla[:, j], *gla_args)
            new_states.append(st)
        yp = _layer_norm(DN_ALPHA * yp + gt_p * hp, ln_g[i], ln_b[i])
        ys = _layer_norm(DN_ALPHA * ys + gt_s * hs, ln_g[i], ln_b[i])
    new_state_gla = jnp.stack(new_states, axis=1)
    return (yp, ys, new_state_gla)
```

```python
import functools
import math

import numpy as np
import jax
import jax.numpy as jnp
from jax import lax
from jax.experimental import pallas as pl
from jax.experimental.pallas import tpu as pltpu

GRID_W = 64
N_MIXERS = 3
LN_EPS = 1e-5
HY_BANDS = 16
HY_TARGET = 1e-2
HY_FAST_PCT = 0.3
HY_SLOW_PCT = 1.5
HY_MIN_DECAY = math.log(HY_TARGET) / HY_SLOW_PCT
HY_MAX_DECAY = math.log(HY_TARGET) / HY_FAST_PCT
GLA_HEADS = 4
GLA_TAU = 16.0
GLA_NORM_EPS = 1e-6

LANES = 128
SUBLANES = 8
MOD_ROWS = 16
VMEM_LIMIT = 56 * 1024 * 1024

F32 = jnp.float32
BF16 = jnp.bfloat16
HIGHEST = lax.Precision.HIGHEST


def _params(*sem):
    return pltpu.CompilerParams(dimension_semantics=sem, vmem_limit_bytes=VMEM_LIMIT)


def _silu(x):
    return x / (1.0 + jnp.exp(-x))


def _mod_kernel(c_ref, w_ref, b_ref, o_ref):
    s = _silu(c_ref[...])
    o_ref[...] = jnp.dot(s, w_ref[...], precision=HIGHEST, preferred_element_type=F32) + b_ref[...]


def _modulation(cond, ada_w, ada_b):
    depth, d, n = ada_w.shape
    tn = 512
    return pl.pallas_call(
        _mod_kernel,
        grid=(depth, n // tn),
        in_specs=[
            pl.BlockSpec((MOD_ROWS, d), lambda i, j: (0, 0)),
            pl.BlockSpec((None, d, tn), lambda i, j: (i, 0, j)),
            pl.BlockSpec((None, 1, tn), lambda i, j: (i, 0, j)),
        ],
        out_specs=pl.BlockSpec((None, MOD_ROWS, tn), lambda i, j: (i, 0, j)),
        out_shape=jax.ShapeDtypeStruct((depth, MOD_ROWS, n), F32),
        compiler_params=_params("parallel", "parallel"),
    )(cond, ada_w, ada_b.reshape(depth, 1, n))


def _premod_kernel(x_ref, sh_ref, sc_ref, u_ref):
    u_ref[...] = (x_ref[...] * (1.0 + sc_ref[...]) + sh_ref[...]).astype(BF16)


def _row_spec(d, row0, stride):
    return pl.BlockSpec((None, 1, d), lambda b, t: (row0 + stride * b, 0, 0))


def _premod(x, shift, scale, row0, stride, tt):
    bsz, length, d = x.shape
    tok = pl.BlockSpec((None, tt, d), lambda b, t: (b, t, 0))
    return pl.pallas_call(
        _premod_kernel,
        grid=(bsz, length // tt),
        in_specs=[tok, _row_spec(d, row0, stride), _row_spec(d, row0, stride)],
        out_specs=tok,
        out_shape=jax.ShapeDtypeStruct(x.shape, BF16),
        compiler_params=_params("parallel", "parallel"),
    )(x, shift, scale)


def _mm_kernel(a_ref, b_ref, o_ref):
    o_ref[...] = jnp.dot(a_ref[...], b_ref[...], preferred_element_type=F32)


def _matmul(a, b, tm=1024, tn=1024):
    m, k = a.shape
    n = b.shape[1]
    tm = min(tm, m)
    tn = min(tn, n)
    assert m % tm == 0 and n % tn == 0
    return pl.pallas_call(
        _mm_kernel,
        grid=(n // tn, m // tm),
        in_specs=[
            pl.BlockSpec((tm, k), lambda j, i: (i, 0)),
            pl.BlockSpec((k, tn), lambda j, i: (0, j)),
        ],
        out_specs=pl.BlockSpec((tm, tn), lambda j, i: (i, j)),
        out_shape=jax.ShapeDtypeStruct((m, n), F32),
        compiler_params=_params("parallel", "parallel"),
    )(a, b)


def _ln_kernel(*refs, alpha, with_next):
    if with_next:
        x_ref, h_ref, gt_ref, g_ref, b_ref, sh_ref, sc_ref, xo_ref, uo_ref = refs
    else:
        x_ref, h_ref, gt_ref, g_ref, b_ref, xo_ref = refs
    y = alpha * x_ref[...] + gt_ref[...] * h_ref[...]
    mu = jnp.mean(y, axis=-1, keepdims=True)
    dlt = y - mu
    var = jnp.mean(dlt * dlt, axis=-1, keepdims=True)
    xn = dlt * lax.rsqrt(var + LN_EPS) * g_ref[...] + b_ref[...]
    xo_ref[...] = xn
    if with_next:
        uo_ref[...] = (xn * (1.0 + sc_ref[...]) + sh_ref[...]).astype(BF16)


def _ln_step(x, h, gate, ln_g, ln_b, nxt, alpha, row0, stride, tt):
    bsz, length, d = x.shape
    tok = pl.BlockSpec((None, tt, d), lambda b, t: (b, t, 0))
    row = _row_spec(d, row0, stride)
    vec = pl.BlockSpec((1, d), lambda b, t: (0, 0))
    ins = [x, h.reshape(x.shape), gate, ln_g.reshape(1, d), ln_b.reshape(1, d)]
    specs = [tok, tok, row, vec, vec]
    out_shape = [jax.ShapeDtypeStruct(x.shape, F32)]
    out_specs = [tok]
    if nxt is not None:
        ins += [nxt[0], nxt[1]]
        specs += [row, row]
        out_shape.append(jax.ShapeDtypeStruct(x.shape, BF16))
        out_specs.append(tok)
    res = pl.pallas_call(
        functools.partial(_ln_kernel, alpha=alpha, with_next=nxt is not None),
        grid=(bsz, length // tt),
        in_specs=specs,
        out_specs=out_specs,
        out_shape=out_shape,
        compiler_params=_params("parallel", "parallel"),
    )(*ins)
    return (res[0], res[1]) if nxt is not None else (res[0], None)


def _conv3_rows(x, w_ref, period):
    rows = x.shape[0]
    pos = lax.broadcasted_iota(jnp.int32, x.shape, 0) % period
    prev = jnp.where(pos == 0, 0.0, pltpu.roll(x, 1, 0))
    nxt = jnp.where(pos == period - 1, 0.0, pltpu.roll(x, rows - 1, 0))
    return prev * w_ref[0:1, :] + x * w_ref[1:2, :] + nxt * w_ref[2:3, :]


def _sconv_kernel(b_ref, c_ref, x_ref, g_ref, w_ref, z_ref, *, period):
    y = b_ref[...] * _conv3_rows(c_ref[...] * x_ref[...], w_ref, period)
    z_ref[...] = (y * _silu(g_ref[...])).astype(BF16)


def _sconv_core(proj, conv_w, period, tt, ct):
    t, four_w = proj.shape
    w = four_w // 4
    nb = w // ct

    def col(k):
        return pl.BlockSpec((tt, ct), lambda i, j, k=k: (i, k * nb + j))

    return pl.pallas_call(
        functools.partial(_sconv_kernel, period=period),
        grid=(t // tt, nb),
        in_specs=[col(0), col(1), col(2), col(3), pl.BlockSpec((3, ct), lambda i, j: (0, j))],
        out_specs=pl.BlockSpec((tt, ct), lambda i, j: (i, j)),
        out_shape=jax.ShapeDtypeStruct((t, w), BF16),
        compiler_params=_params("parallel", "parallel"),
    )(proj, proj, proj, proj, conv_w)


def _hy_pre_kernel(p0_ref, p1_ref, p2_ref, pg_ref, w0_ref, w1_ref, w2_ref,
                   b0_ref, b1_ref, b2_ref, v_ref, xg_ref, *, period):
    x0 = _conv3_rows(p0_ref[...], w0_ref, period) + b0_ref[...]
    x1 = _conv3_rows(p1_ref[...], w1_ref, period) + b1_ref[...]
    v = _conv3_rows(p2_ref[...], w2_ref, period) + b2_ref[...]
    v_ref[...] = v * x1
    xg_ref[...] = x0 * _silu(pg_ref[...])


def _hy_pre(proj, conv_w, conv_b, period, tt, ct):
    t, four_w = proj.shape
    w = four_w // 4
    nb = w // ct

    def col(k, rows):
        return pl.BlockSpec((rows, ct), lambda i, j, k=k: ((i if rows == tt else 0), k * nb + j))

    out = pl.BlockSpec((tt, ct), lambda i, j: (i, j))
    return pl.pallas_call(
        functools.partial(_hy_pre_kernel, period=period),
        grid=(t // tt, nb),
        in_specs=[col(0, tt), col(1, tt), col(2, tt), col(3, tt),
                  col(0, 3), col(1, 3), col(2, 3), col(0, 1), col(1, 1), col(2, 1)],
        out_specs=[out, out],
        out_shape=[jax.ShapeDtypeStruct((t, w), F32), jax.ShapeDtypeStruct((t, w), F32)],
        compiler_params=_params("parallel", "parallel"),
    )(proj, proj, proj, proj, conv_w, conv_w, conv_w,
      conv_b.reshape(1, -1), conv_b.reshape(1, -1), conv_b.reshape(1, -1))


def _filter_kernel(band_ref, sel_ref, w_in_ref, b_in_ref, w_mid_ref, b_mid_ref, fr_ref,
                   w_out_ref, b_out_ref, dl_ref, k_ref, *, length, tr):
    n = pl.program_id(0) * tr + lax.broadcasted_iota(jnp.int32, (tr, 1), 0)
    pos = jnp.where(n < length, n, 2 * length - n)
    t = pos.astype(F32) / length
    ang = (2.0 * math.pi) * t * band_ref[...]
    sel = sel_ref[...]
    z = jnp.where(sel == 0, t, jnp.where(sel == 1, jnp.cos(ang), jnp.where(sel == 2, -jnp.sin(ang), 0.0)))
    fr = fr_ref[...]
    h = jnp.sin(fr * (jnp.dot(z, w_in_ref[...], precision=HIGHEST, preferred_element_type=F32) + b_in_ref[...]))
    for m in range(w_mid_ref.shape[0]):
        h = jnp.sin(fr * (jnp.dot(h, w_mid_ref[m], precision=HIGHEST, preferred_element_type=F32)
                          + b_mid_ref[m]))
    o = jnp.dot(h, w_out_ref[...], precision=HIGHEST, preferred_element_type=F32) + b_out_ref[...]
    o = o * jnp.exp(-t * dl_ref[...])
    k_ref[...] = jnp.where(n == length, 0.0, o)


def _hyena_filter(length, width, fw_in, fb_in, fw_mid, fb_mid, freq, fw_out, fb_out):
    hid = fw_in.shape[1]
    n_mid = fw_mid.shape[0]
    emb = fw_in.shape[0]
    hp = LANES
    bands = jnp.linspace(1e-4, HY_BANDS - 1, HY_BANDS, dtype=F32)
    band_row = jnp.zeros((1, hp), F32).at[0, 1:1 + HY_BANDS].set(bands).at[0, 1 + HY_BANDS:emb].set(bands)
    sel_np = np.full((1, hp), 3, np.int32)
    sel_np[0, 0] = 0
    sel_np[0, 1:1 + HY_BANDS] = 1
    sel_np[0, 1 + HY_BANDS:emb] = 2
    w_in = jnp.zeros((hp, hp), F32).at[:emb, :hid].set(fw_in)
    b_in = jnp.zeros((1, hp), F32).at[0, :hid].set(fb_in)
    w_mid = jnp.zeros((n_mid, hp, hp), F32).at[:, :hid, :hid].set(fw_mid)
    b_mid = jnp.zeros((n_mid, 1, hp), F32).at[:, 0, :hid].set(fb_mid)
    fr = jnp.zeros((1, hp), F32).at[0, :hid].set(freq)
    w_out = jnp.zeros((hp, 2 * width), F32).at[:hid].set(fw_out)
    deltas = jnp.abs(jnp.linspace(HY_MIN_DECAY, HY_MAX_DECAY, width, dtype=F32)).reshape(1, width)
    n_rows = 2 * length
    tr = min(256, length)
    half = length // tr
    const2 = lambda i: (0, 0)
    const3 = lambda i: (0, 0, 0)
    side = lambda i: (0, jnp.where(i < half, 0, 1))
    return pl.pallas_call(
        functools.partial(_filter_kernel, length=length, tr=tr),
        grid=(n_rows // tr,),
        in_specs=[
            pl.BlockSpec((1, hp), const2), pl.BlockSpec((1, hp), const2),
            pl.BlockSpec((hp, hp), const2), pl.BlockSpec((1, hp), const2),
            pl.BlockSpec((n_mid, hp, hp), const3), pl.BlockSpec((n_mid, 1, hp), const3),
            pl.BlockSpec((1, hp), const2),
            pl.BlockSpec((hp, width), side), pl.BlockSpec((1, width), side),
            pl.BlockSpec((1, width), const2),
        ],
        out_specs=pl.BlockSpec((tr, width), lambda i: (i, 0)),
        out_shape=jax.ShapeDtypeStruct((n_rows, width), F32),
        compiler_params=_params("parallel"),
    )(band_row, jnp.asarray(sel_np), w_in, b_in, w_mid, b_mid, fr, w_out,
      fb_out.reshape(1, 2 * width), deltas)


def _fft_split(n):
    if n >= 8192:
        return 64, n // 64
    if n >= 512:
        return 32, n // 32
    return 16, n // 16


def _dft_tables(n1, n2):
    n = n1 * n2
    h1 = n1 // 2
    k1 = np.arange(n1, dtype=np.float64)
    def stage1(cols):
        th = 2 * np.pi * (k1[None, :, None] * np.arange(cols)[None, None, :] / n1
                          + np.arange(n2)[:, None, None] * k1[None, :, None] / n)
        return np.cos(th), np.sin(th)
    c, s = stage1(h1)
    t1 = np.concatenate([np.concatenate([c, s], 2), np.concatenate([-s, c], 2)], 1)
    c, s = stage1(n1)
    t1f = np.concatenate([c, -s], 1)
    ph = 2 * np.pi * np.outer(np.arange(n2), np.arange(n2)) / n2
    c, s = np.cos(ph), np.sin(ph)
    t2 = np.block([[c, s], [-s, c]])
    ps = 2 * np.pi * (np.outer(np.arange(n2), np.arange(n2))[None] / n2
                      + k1[:, None, None] * np.arange(n2)[None, :, None] / n)
    c, s = np.cos(ps), np.sin(ps)
    t2i = np.concatenate([np.concatenate([c, -s], 2), np.concatenate([s, c], 2)], 1)
    th = 2 * np.pi * np.outer(np.arange(h1), k1) / n1
    c, s = np.cos(th) / n, np.sin(th) / n
    t3 = np.block([[c, -s], [s, c]])
    f = lambda a: jnp.asarray(a, dtype=F32)
    return f(t1), f(t1f), f(t2), f(t2i), f(t3)


def _hdot(a, b):
    return jnp.dot(a, b, precision=HIGHEST, preferred_element_type=F32)


def _fft_s1_kernel(*refs):
    t_ref, o_ref = refs[-2], refs[-1]
    x = jnp.concatenate([r[...] for r in refs[:-2]], axis=0) if len(refs) > 3 else refs[0][...]
    o_ref[...] = _hdot(t_ref[...], x)


def _fft_stage1(parts, table, pairs, n1, n2, width):
    rows = parts[0].shape[1]
    step = len(parts)

    def spec(q):
        return pl.BlockSpec((None, rows, width), lambda p, j, q=q: (step * p + q if step > 1 else p, 0, j))

    srcs = [parts[0]] * step if step > 1 else parts
    return pl.pallas_call(
        _fft_s1_kernel,
        grid=(pairs, n2),
        in_specs=[spec(q) for q in range(step)]
        + [pl.BlockSpec((None, 2 * n1, table.shape[2]), lambda p, j: (j, 0, 0))],
        out_specs=pl.BlockSpec((None, 2 * n1, width), lambda p, j: (p, 0, j)),
        out_shape=jax.ShapeDtypeStruct((pairs, 2 * n1, n2 * width), F32),
        compiler_params=_params("parallel", "parallel"),
    )(*srcs, table)


def _fft_spec_kernel(x_ref, t_ref, o_ref):
    n2 = x_ref.shape[1]
    x = x_ref[...].reshape(2 * n2, x_ref.shape[2])
    o_ref[...] = _hdot(t_ref[...], x).reshape(o_ref.shape)


def _fft_filter_spectrum(a, t2, n1, n2, width, wt):
    a5 = a.reshape(1, 2, n1, n2, width)
    return pl.pallas_call(
        _fft_spec_kernel,
        grid=(n1, width // wt),
        in_specs=[pl.BlockSpec((None, 2, None, n2, wt), lambda k, j: (0, 0, k, 0, j)),
                  pl.BlockSpec((2 * n2, 2 * n2), lambda k, j: (0, 0))],
        out_specs=pl.BlockSpec((2, None, n2, wt), lambda k, j: (0, k, 0, j)),
        out_shape=jax.ShapeDtypeStruct((2, n1, n2, width), F32),
        compiler_params=_params("parallel", "parallel"),
    )(a5, t2)


def _fft_mid_kernel(x_ref, kf_ref, t2_ref, t2i_ref, o_ref):
    n2 = x_ref.shape[1]
    x = x_ref[...].reshape(2 * n2, x_ref.shape[2])
    s = _hdot(t2_ref[...], x)
    sr, si = s[:n2], s[n2:]
    kr, ki = kf_ref[0], kf_ref[1]
    y = jnp.concatenate([sr * kr - si * ki, sr * ki + si * kr], axis=0)
    o_ref[...] = _hdot(t2i_ref[...], y).reshape(o_ref.shape)


def _fft_mid(a, kf, t2, t2i, pairs, n1, n2, width, wt):
    a5 = a.reshape(pairs, 2, n1, n2, width)
    blk = pl.BlockSpec((None, 2, None, n2, wt), lambda k, j, p: (p, 0, k, 0, j))
    out = pl.pallas_call(
        _fft_mid_kernel,
        grid=(n1, width // wt, pairs),
        in_specs=[blk,
                  pl.BlockSpec((2, None, n2, wt), lambda k, j, p: (0, k, 0, j)),
                  pl.BlockSpec((2 * n2, 2 * n2), lambda k, j, p: (0, 0)),
                  pl.BlockSpec((None, 2 * n2, 2 * n2), lambda k, j, p: (k, 0, 0))],
        out_specs=blk,
        out_shape=jax.ShapeDtypeStruct((pairs, 2, n1, n2, width), F32),
        compiler_params=_params("parallel", "parallel", "parallel"),
    )(a5, kf, t2, t2i)
    return out.reshape(pairs, 2 * n1, n2 * width)


def _fft_s3_kernel(b_ref, t_ref, v0_ref, v1_ref, g0_ref, g1_ref, sk_ref, z_ref):
    h1 = v0_ref.shape[0]
    y = _hdot(t_ref[...], b_ref[...])
    sk = sk_ref[...]
    z_ref[0] = (g0_ref[...] * (y[:h1] + v0_ref[...] * sk)).astype(BF16)
    z_ref[1] = (g1_ref[...] * (y[h1:] + v1_ref[...] * sk)).astype(BF16)


def _fft_stage3(bm, t3, vprime, xg, skip, pairs, n1, n2, width):
    h1 = n1 // 2
    bsz = 2 * pairs
    v3 = vprime.reshape(bsz, h1, n2 * width)
    g3 = xg.reshape(bsz, h1, n2 * width)

    def tok(q):
        return pl.BlockSpec((None, h1, width), lambda p, j, q=q: (2 * p + q, 0, j))

    z = pl.pallas_call(
        _fft_s3_kernel,
        grid=(pairs, n2),
        in_specs=[pl.BlockSpec((None, 2 * n1, width), lambda p, j: (p, 0, j)),
                  pl.BlockSpec((n1, 2 * n1), lambda p, j: (0, 0)),
                  tok(0), tok(1), tok(0), tok(1),
                  pl.BlockSpec((1, width), lambda p, j: (0, 0))],
        out_specs=pl.BlockSpec((None, 2, h1, width), lambda p, j: (p, 0, 0, j)),
        out_shape=jax.ShapeDtypeStruct((pairs, 2, h1, n2 * width), BF16),
        compiler_params=_params("parallel", "parallel"),
    )(bm, t3, v3, v3, g3, g3, skip.reshape(1, width))
    return z.reshape(bsz * h1 * n2, width)


def _hyena_core(proj, bsz, length, period, hy, tt, ct):
    conv_w, conv_b, fw_in, fb_in, fw_mid, fb_mid, freq, fw_out, fb_out, skip = hy
    width = proj.shape[1] // 4
    vprime, xg = _hy_pre(proj, conv_w, conv_b, period, tt, ct)
    n = 2 * length
    n1, n2 = _fft_split(n)
    t1, t1f, t2, t2i, t3 = _dft_tables(n1, n2)
    kfull = _hyena_filter(length, width, fw_in, fb_in, fw_mid, fb_mid, freq, fw_out, fb_out)
    wt = min(1024, width)
    ka = _fft_stage1([kfull.reshape(1, n1, n2 * width)], t1f, 1, n1, n2, width)
    kf = _fft_filter_spectrum(ka, t2, n1, n2, width, wt)
    pairs = bsz // 2
    v3 = vprime.reshape(bsz, n1 // 2, n2 * width)
    a = _fft_stage1([v3, v3], t1, pairs, n1, n2, width)
    bm = _fft_mid(a, kf, t2, t2i, pairs, n1, n2, width, wt)
    return _fft_stage3(bm, t3, vprime, xg, skip, pairs, n1, n2, width)


def _gla_gate_kernel(low_ref, w_ref, b_ref, g_ref):
    logit = _hdot(low_ref[...], w_ref[...]) + b_ref[...]
    ls = jnp.minimum(logit, 0.0) - jnp.log(1.0 + jnp.exp(-jnp.abs(logit)))
    half = g_ref.shape[2]
    g_ref[0] = ls[:, :half] / GLA_TAU
    g_ref[1] = ls[:, half:] / GLA_TAU


def _gla_gates(low, wg2, bg, tt):
    t, lw = low.shape
    rank, dk_tot = wg2.shape[1], wg2.shape[2]
    wfull = jnp.zeros((lw, 2 * dk_tot), F32)
    wfull = wfull.at[:rank, :dk_tot].set(wg2[0]).at[rank:2 * rank, dk_tot:].set(wg2[1])
    return pl.pallas_call(
        _gla_gate_kernel,
        grid=(t // tt,),
        in_specs=[pl.BlockSpec((tt, lw), lambda i: (i, 0)),
                  pl.BlockSpec((lw, 2 * dk_tot), lambda i: (0, 0)),
                  pl.BlockSpec((1, 2 * dk_tot), lambda i: (0, 0))],
        out_specs=pl.BlockSpec((2, tt, dk_tot), lambda i: (0, i, 0)),
        out_shape=jax.ShapeDtypeStruct((2, t, dk_tot), F32),
        compiler_params=_params("parallel"),
    )(low, wfull, bg.reshape(1, 2 * dk_tot))


GLA_DIAG = 8


def _gla_levels(cc):
    lv = []
    m = cc // 2
    while m >= GLA_DIAG:
        lv.append(m)
        m //= 2
    return lv


def _gla_level_map(cc, reverse):
    t = np.arange(cc)[:, None]
    s = np.arange(cc)[None, :]
    out = np.full((cc, cc), -1, np.int32)
    for idx, m in enumerate(_gla_levels(cc)):
        same = (t // (2 * m)) == (s // (2 * m))
        t_hi = (t % (2 * m)) >= m
        s_hi = (s % (2 * m)) >= m
        own = same & (t_hi & ~s_hi if not reverse else ~t_hi & s_hi)
        out[own] = idx
    return out


def _gla_scan_kernel(*refs, reverse, has_init, scale, cc):
    if has_init:
        q_ref, k_ref, v_ref, g_ref, lv_ref, s0_ref, o_ref, sf_ref, st_ref, b_ref = refs
    else:
        q_ref, k_ref, v_ref, g_ref, lv_ref, o_ref, sf_ref, st_ref, b_ref = refs
    c = pl.program_id(2)
    dk = q_ref.shape[1]

    @pl.when(c == 0)
    def _():
        if has_init:
            st_ref[...] = s0_ref[...]
        else:
            st_ref[...] = jnp.zeros_like(st_ref)

    row = lax.broadcasted_iota(jnp.int32, (cc, dk), 0)
    b = g_ref[...]
    k = 1
    while k < cc:
        if reverse:
            b = b + jnp.where(row < cc - k, pltpu.roll(b, cc - k, 0), 0.0)
        else:
            b = b + jnp.where(row >= k, pltpu.roll(b, k, 0), 0.0)
        k *= 2
    b_ref[...] = b
    q = q_ref[...] * scale
    kk = k_ref[...]
    vb = v_ref[...].astype(BF16)
    state = st_ref[...]

    edge = 0 if reverse else cc - 1
    b_edge = b_ref[edge:edge + 1, :]
    o = lax.dot_general((q * jnp.exp(b)).astype(BF16), state.astype(BF16),
                        (((1,), (1,)), ((), ())), preferred_element_type=F32)
    kd = (kk * jnp.exp(b_edge - b)).astype(BF16)
    upd = lax.dot_general(vb, kd, (((0,), (0,)), ((), ())), preferred_element_type=F32)
    st_ref[...] = state * jnp.exp(b_edge) + upd

    lvm = lv_ref[...]
    attn = jnp.zeros((cc, cc), F32)
    for idx, m in enumerate(_gla_levels(cc)):
        pieces = []
        for blk in range(cc // (2 * m)):
            base = blk * 2 * m
            ref_row = base + (m if reverse else m - 1)
            bb = b_ref[base:base + 2 * m, :]
            rr = b_ref[ref_row:ref_row + 1, :]
            hi = lax.broadcasted_iota(jnp.int32, (2 * m, dk), 0) >= m
            qside = (~hi) if reverse else hi
            pieces.append(jnp.where(qside, bb - rr, rr - bb))
        x = pieces[0] if len(pieces) == 1 else jnp.concatenate(pieces, axis=0)
        w = jnp.exp(jnp.minimum(x, 0.0))
        p = lax.dot_general((q * w).astype(BF16), (kk * w).astype(BF16),
                            (((1,), (1,)), ((), ())), preferred_element_type=F32)
        attn = jnp.where(lvm == idx, p, attn)

    dg = GLA_DIAG
    lane = lax.broadcasted_iota(jnp.int32, (dg, cc), 1)
    sub = lax.broadcasted_iota(jnp.int32, (dg, 1), 0)
    strips = []
    for blk in range(cc // dg):
        base = blk * dg
        bb = b_ref[base:base + dg, :]
        qq = q[base:base + dg, :]
        strip = jnp.zeros((dg, cc), F32)
        for j in range(dg):
            bj = b_ref[base + j:base + j + 1, :]
            kj = kk[base + j:base + j + 1, :]
            term = qq * kj * jnp.exp(jnp.minimum(bb - bj, 0.0))
            col = jnp.sum(term, axis=1, keepdims=True)
            keep = (sub <= j) if reverse else (sub >= j)
            col = jnp.where(keep, col, 0.0)
            strip = jnp.where(lane == base + j, col, strip)
        strips.append(strip)
    diag = jnp.concatenate(strips, axis=0)
    attn = jnp.where(lvm < 0, diag, attn)
    o = o + jnp.dot(attn.astype(BF16), vb, preferred_element_type=F32)
    o_ref[...] = o

    @pl.when(c == pl.num_programs(2) - 1)
    def _():
        sf_ref[...] = st_ref[...].T


def _gla_scan(proj, g, init, bsz, length, dk, dv, direction, cc):
    reverse = direction == 1
    nh = GLA_HEADS
    nc = length // cc
    t = bsz * length
    kblk = (nh * dk) // dk
    vblk = (2 * nh * dk) // dv

    def chunk(b, c):
        return b * nc + (nc - 1 - c if reverse else c)

    in_specs = [
        pl.BlockSpec((cc, dk), lambda b, h, c: (chunk(b, c), h)),
        pl.BlockSpec((cc, dk), lambda b, h, c: (chunk(b, c), kblk + h)),
        pl.BlockSpec((cc, dv), lambda b, h, c: (chunk(b, c), vblk + h)),
        pl.BlockSpec((None, cc, dk), lambda b, h, c: (direction, chunk(b, c), h)),
        pl.BlockSpec((cc, cc), lambda b, h, c: (0, 0)),
    ]
    ins = [proj, proj, proj, g, jnp.asarray(_gla_level_map(cc, reverse))]
    if init is not None:
        in_specs.append(pl.BlockSpec((None, None, dv, dk), lambda b, h, c: (b, h, 0, 0)))
        ins.append(init)
    o, sf = pl.pallas_call(
        functools.partial(_gla_scan_kernel, reverse=reverse, has_init=init is not None,
                          scale=dk ** -0.5, cc=cc),
        grid=(bsz, nh, nc),
        in_specs=in_specs,
        out_specs=[pl.BlockSpec((cc, dv), lambda b, h, c: (chunk(b, c), h)),
                   pl.BlockSpec((None, None, dk, dv), lambda b, h, c: (b, h, 0, 0))],
        out_shape=[jax.ShapeDtypeStruct((t, nh * dv), F32),
                   jax.ShapeDtypeStruct((bsz, nh, dk, dv), F32)],
        scratch_shapes=[pltpu.VMEM((dv, dk), F32), pltpu.VMEM((cc, dk), F32)],
        compiler_params=_params("parallel", "parallel", "arbitrary"),
    )(*ins)
    return o, sf


def _gla_out_kernel(of_ref, ob_ref, r_ref, ng_ref, z_ref):
    o = of_ref[...] + ob_ref[...]
    o = o * lax.rsqrt(jnp.mean(o * o, axis=-1, keepdims=True) + GLA_NORM_EPS)
    z_ref[...] = (o * ng_ref[...] * _silu(r_ref[...])).astype(BF16)


def _gla_out(o_f, o_b, proj, norm_g, dk, dv, tt):
    t = o_f.shape[0]
    nh = GLA_HEADS
    rblk = (2 * nh * dk + nh * dv) // dv
    blk = pl.BlockSpec((tt, dv), lambda i, h: (i, h))
    return pl.pallas_call(
        _gla_out_kernel,
        grid=(t // tt, nh),
        in_specs=[blk, blk, pl.BlockSpec((tt, dv), lambda i, h: (i, rblk + h)),
                  pl.BlockSpec((1, dv), lambda i, h: (0, 0))],
        out_specs=blk,
        out_shape=jax.ShapeDtypeStruct((t, nh * dv), BF16),
        compiler_params=_params("parallel", "parallel"),
    )(o_f, o_b, proj, norm_g.reshape(1, dv))


def _transpose_kernel(x_ref, o_ref):
    o_ref[...] = x_ref[...].T


def _transpose_states(s):
    bsz, nh, dk, dv = s.shape
    return pl.pallas_call(
        _transpose_kernel,
        grid=(bsz, nh),
        in_specs=[pl.BlockSpec((None, None, dk, dv), lambda b, h: (b, h, 0, 0))],
        out_specs=pl.BlockSpec((None, None, dv, dk), lambda b, h: (b, h, 0, 0)),
        out_shape=jax.ShapeDtypeStruct((bsz, nh, dv, dk), F32),
        compiler_params=_params("parallel", "parallel"),
    )(s)


def _gla_core(u, proj, bsz, length, init, gla, tt):
    wg1, wg2, bg, norm_g = gla
    d = u.shape[1]
    rank = wg1.shape[2]
    dk_tot = wg2.shape[2]
    dk = dk_tot // GLA_HEADS
    dv = norm_g.shape[0]
    wlow = jnp.zeros((d, LANES), F32).at[:, :rank].set(wg1[0]).at[:, rank:2 * rank].set(wg1[1])
    low = _matmul(u, wlow.astype(BF16))
    g = _gla_gates(low, wg2, bg, tt)
    cc = min(256, length)
    outs, finals = [], []
    for direction in (0, 1):
        s0 = None if init is None else _transpose_states(init[:, direction])
        o, sf = _gla_scan(proj, g, s0, bsz, length, dk, dv, direction, cc)
        outs.append(o)
        finals.append(sf)
    z = _gla_out(outs[0], outs[1], proj, norm_g, dk, dv, tt)
    return z, finals


def kernel(x_prompt, x_sample, state_gla, c, c_ctx, ada_w, ada_b, ln_g, ln_b,
           hy_w_in, hy_conv_w, hy_conv_b, hy_fw_in, hy_fb_in, hy_fw_mid, hy_fb_mid,
           hy_freq, hy_fw_out, hy_fb_out, hy_skip, hy_w_out,
           sc_w_in, sc_conv_w, sc_w_out,
           gla_w_in, gla_wg1, gla_wg2, gla_bg, gla_norm_g, gla_w_out):
    depth, d, _ = ada_w.shape
    alpha = (2 * depth) ** 0.25
    bp, lp, _ = x_prompt.shape
    bs, ls, _ = x_sample.shape
    assert 1 + bs <= MOD_ROWS
    grid_w = min(GRID_W, ls)
    tt_p = min(256, lp)
    tt_s = min(256, ls)
    ct = min(512, d)

    cond = jnp.zeros((MOD_ROWS, d), F32).at[0].set(c_ctx).at[1:1 + bs].set(c)
    mods = _modulation(cond, ada_w, ada_b)
    mods = mods.reshape(depth, MOD_ROWS, 3, 1, d).transpose(0, 2, 1, 3, 4)

    streams = [
        dict(x=x_prompt, b=bp, l=lp, period=lp, row0=0, stride=0, tt=tt_p),
        dict(x=x_sample, b=bs, l=ls, period=grid_w, row0=1, stride=1, tt=tt_s),
    ]
    for s in streams:
        s["u"] = _premod(s["x"], mods[0, 0], mods[0, 1], s["row0"], s["stride"], s["tt"])

    new_states = []
    for i in range(depth):
        kind = i % N_MIXERS
        j = i // N_MIXERS
        if kind == 0:
            w_in, w_out = hy_w_in[j].astype(BF16), hy_w_out[j].astype(BF16)
            hy = (hy_conv_w[j], hy_conv_b[j], hy_fw_in[j], hy_fb_in[j], hy_fw_mid[j], hy_fb_mid[j],
                  hy_freq[j], hy_fw_out[j], hy_fb_out[j], hy_skip[j])
        elif kind == 1:
            w_in, w_out = sc_w_in[j].astype(BF16), sc_w_out[j].astype(BF16)
        else:
            w_in, w_out = gla_w_in[j].astype(BF16), gla_w_out[j].astype(BF16)
            gla = (gla_wg1[j], gla_wg2[j], gla_bg[j], gla_norm_g[j])
        for si, s in enumerate(streams):
            t = s["b"] * s["l"]
            u2 = s["u"].reshape(t, d)
            proj = _matmul(u2, w_in)
            if kind == 0:
                z = _hyena_core(proj, s["b"], s["l"], s["period"], hy, s["tt"], ct)
            elif kind == 1:
                z = _sconv_core(proj, sc_conv_w[j], s["period"], s["tt"], ct)
            else:
                init = None if si == 0 else state_gla[:, j]
                z, finals = _gla_core(u2, proj, s["b"], s["l"], init, gla, s["tt"])
                if si == 0:
                    new_states.append(jnp.stack(finals, axis=1).astype(x_prompt.dtype))
            h = _matmul(z, w_out)
            nxt = None if i + 1 == depth else (mods[i + 1, 0], mods[i + 1, 1])
            s["x"], s["u"] = _ln_step(s["x"], h, mods[i, 2], ln_g[i], ln_b[i], nxt, alpha,
                                      s["row0"], s["stride"], s["tt"])
    return streams[0]["x"], streams[1]["x"], jnp.stack(new_states, axis=1)
```

```python
import functools
import math

import numpy as np
import jax
import jax.numpy as jnp
from jax import lax
from jax.experimental import pallas as pl
from jax.experimental.pallas import tpu as pltpu

GRID_W = 64
N_MIXERS = 3
LN_EPS = 1e-5
HY_BANDS = 16
HY_TARGET = 1e-2
HY_FAST_PCT = 0.3
HY_SLOW_PCT = 1.5
HY_MIN_DECAY = math.log(HY_TARGET) / HY_SLOW_PCT
HY_MAX_DECAY = math.log(HY_TARGET) / HY_FAST_PCT
GLA_HEADS = 4
GLA_TAU = 16.0
GLA_NORM_EPS = 1e-6

LANES = 128
SUBLANES = 8
MOD_ROWS = 16
VMEM_LIMIT = 56 * 1024 * 1024

TM = 1024
TN = 1024
CT_FUSED = 256
TT = 256
WT_FFT = 1024
FFT_S = SUBLANES
FFT_DIRECT_MAX = 512
GLA_CHUNK = 256
GLA_DIAG = 8

F32 = jnp.float32
BF16 = jnp.bfloat16
HIGHEST = lax.Precision.HIGHEST


def _params(*sem):
    return pltpu.CompilerParams(dimension_semantics=sem, vmem_limit_bytes=VMEM_LIMIT)


def _silu(x):
    return x / (1.0 + jnp.exp(-x))


def _mod_kernel(c_ref, w_ref, b_ref, o_ref):
    s = _silu(c_ref[...])
    o_ref[...] = jnp.dot(s, w_ref[...], precision=HIGHEST, preferred_element_type=F32) + b_ref[...]


def _modulation(cond, ada_w, ada_b):
    depth, d, n = ada_w.shape
    tn = 512
    return pl.pallas_call(
        _mod_kernel,
        grid=(depth, n // tn),
        in_specs=[
            pl.BlockSpec((MOD_ROWS, d), lambda i, j: (0, 0)),
            pl.BlockSpec((None, d, tn), lambda i, j: (i, 0, j)),
            pl.BlockSpec((None, 1, tn), lambda i, j: (i, 0, j)),
        ],
        out_specs=pl.BlockSpec((None, MOD_ROWS, tn), lambda i, j: (i, 0, j)),
        out_shape=jax.ShapeDtypeStruct((depth, MOD_ROWS, n), F32),
        compiler_params=_params("parallel", "parallel"),
    )(cond, ada_w, ada_b.reshape(depth, 1, n))


def _premod_kernel(x_ref, sh_ref, sc_ref, u_ref):
    u_ref[...] = (x_ref[...] * (1.0 + sc_ref[...]) + sh_ref[...]).astype(BF16)


def _row_spec(d, row0, stride):
    return pl.BlockSpec((None, 1, d), lambda b, t: (row0 + stride * b, 0, 0))


def _premod(x, shift, scale, row0, stride, tt):
    bsz, length, d = x.shape
    tok = pl.BlockSpec((None, tt, d), lambda b, t: (b, t, 0))
    return pl.pallas_call(
        _premod_kernel,
        grid=(bsz, length // tt),
        in_specs=[tok, _row_spec(d, row0, stride), _row_spec(d, row0, stride)],
        out_specs=tok,
        out_shape=jax.ShapeDtypeStruct(x.shape, BF16),
        compiler_params=_params("parallel", "parallel"),
    )(x, shift, scale)


def _mm_kernel(a_ref, b_ref, o_ref):
    o_ref[...] = jnp.dot(a_ref[...].astype(BF16), b_ref[...], preferred_element_type=F32)


def _matmul(a, b):
    m, k = a.shape
    n = b.shape[1]
    tm = min(TM if a.dtype == BF16 else TM // 2, m)
    tn = min(TN, n)
    assert m % tm == 0 and n % tn == 0
    return pl.pallas_call(
        _mm_kernel,
        grid=(n // tn, m // tm),
        in_specs=[
            pl.BlockSpec((tm, k), lambda j, i: (i, 0)),
            pl.BlockSpec((k, tn), lambda j, i: (0, j)),
        ],
        out_specs=pl.BlockSpec((tm, tn), lambda j, i: (i, j)),
        out_shape=jax.ShapeDtypeStruct((m, n), F32),
        compiler_params=_params("parallel", "parallel"),
    )(a, b)


def _conv3_rows(x, w_ref, period):
    rows = x.shape[0]
    pos = lax.broadcasted_iota(jnp.int32, x.shape, 0) % period
    prev = jnp.where(pos == 0, 0.0, pltpu.roll(x, 1, 0))
    nxt = jnp.where(pos == period - 1, 0.0, pltpu.roll(x, rows - 1, 0))
    return prev * w_ref[0:1, :] + x * w_ref[1:2, :] + nxt * w_ref[2:3, :]


def _inproj_sconv_kernel(a_ref, wb_ref, wc_ref, wx_ref, wg_ref, cw_ref, z_ref, *, period):
    a = a_ref[...]
    dot = lambda w_ref: jnp.dot(a, w_ref[...], preferred_element_type=F32)
    y = dot(wb_ref) * _conv3_rows(dot(wc_ref) * dot(wx_ref), cw_ref, period)
    z_ref[...] = (y * _silu(dot(wg_ref))).astype(BF16)


def _inproj_hyena_kernel(a_ref, w0_ref, w1_ref, w2_ref, wg_ref, c0_ref, c1_ref, c2_ref,
                         b0_ref, b1_ref, b2_ref, v_ref, xg_ref, *, period):
    a = a_ref[...]
    dot = lambda w_ref: jnp.dot(a, w_ref[...], preferred_element_type=F32)
    x1 = _conv3_rows(dot(w1_ref), c1_ref, period) + b1_ref[...]
    v = _conv3_rows(dot(w2_ref), c2_ref, period) + b2_ref[...]
    v_ref[...] = v * x1
    x0 = _conv3_rows(dot(w0_ref), c0_ref, period) + b0_ref[...]
    xg_ref[...] = x0 * _silu(dot(wg_ref))


def _inproj_fused(u, w_in, kind, conv_w, conv_b, period):
    t, d = u.shape
    w = w_in.shape[1] // 4
    ct = min(CT_FUSED, w)
    tm = min(TM, t)
    nb = w // ct
    assert t % tm == 0 and tm % period == 0

    def col(rows, k):
        return pl.BlockSpec((rows, ct), lambda j, i, k=k: (0, k * nb + j))

    a_spec = pl.BlockSpec((tm, d), lambda j, i: (i, 0))
    out = pl.BlockSpec((tm, ct), lambda j, i: (i, j))
    w_specs = [col(d, k) for k in range(4)]
    if kind == "sconv":
        return pl.pallas_call(
            functools.partial(_inproj_sconv_kernel, period=period),
            grid=(nb, t // tm),
            in_specs=[a_spec] + w_specs + [col(3, 0)],
            out_specs=out,
            out_shape=jax.ShapeDtypeStruct((t, w), BF16),
            compiler_params=_params("parallel", "parallel"),
        )(u, w_in, w_in, w_in, w_in, conv_w)
    cb = conv_b.reshape(1, 3 * w)
    return pl.pallas_call(
        functools.partial(_inproj_hyena_kernel, period=period),
        grid=(nb, t // tm),
        in_specs=[a_spec] + w_specs + [col(3, k) for k in range(3)] + [col(1, k) for k in range(3)],
        out_specs=[out, out],
        out_shape=[jax.ShapeDtypeStruct((t, w), F32)] * 2,
        compiler_params=_params("parallel", "parallel"),
    )(u, w_in, w_in, w_in, w_in, conv_w, conv_w, conv_w, cb, cb, cb)


def _ln_kernel(*refs, alpha, with_next):
    if with_next:
        x_ref, h_ref, gt_ref, g_ref, b_ref, sh_ref, sc_ref, xo_ref, uo_ref = refs
    else:
        x_ref, h_ref, gt_ref, g_ref, b_ref, xo_ref = refs
    y = alpha * x_ref[...] + gt_ref[...] * h_ref[...]
    mu = jnp.mean(y, axis=-1, keepdims=True)
    dlt = y - mu
    var = jnp.mean(dlt * dlt, axis=-1, keepdims=True)
    xn = dlt * lax.rsqrt(var + LN_EPS) * g_ref[...] + b_ref[...]
    xo_ref[...] = xn
    if with_next:
        uo_ref[...] = (xn * (1.0 + sc_ref[...]) + sh_ref[...]).astype(BF16)


def _ln_step(x, h, gate, ln_g, ln_b, nxt, alpha, row0, stride, tt):
    bsz, length, d = x.shape
    tok = pl.BlockSpec((None, tt, d), lambda b, t: (b, t, 0))
    row = _row_spec(d, row0, stride)
    vec = pl.BlockSpec((1, d), lambda b, t: (0, 0))
    ins = [x, h.reshape(x.shape), gate, ln_g.reshape(1, d), ln_b.reshape(1, d)]
    specs = [tok, tok, row, vec, vec]
    out_shape = [jax.ShapeDtypeStruct(x.shape, F32)]
    out_specs = [tok]
    if nxt is not None:
        ins += [nxt[0], nxt[1]]
        specs += [row, row]
        out_shape.append(jax.ShapeDtypeStruct(x.shape, BF16))
        out_specs.append(tok)
    res = pl.pallas_call(
        functools.partial(_ln_kernel, alpha=alpha, with_next=nxt is not None),
        grid=(bsz, length // tt),
        in_specs=specs,
        out_specs=out_specs,
        out_shape=out_shape,
        compiler_params=_params("parallel", "parallel"),
    )(*ins)
    return (res[0], res[1]) if nxt is not None else (res[0], None)


def _filter_kernel(band_ref, sel_ref, w_in_ref, b_in_ref, w_mid_ref, b_mid_ref, fr_ref,
                   w_out_ref, b_out_ref, dl_ref, k_ref, *, length, tr):
    n = pl.program_id(0) * tr + lax.broadcasted_iota(jnp.int32, (tr, 1), 0)
    pos = jnp.where(n < length, n, 2 * length - n)
    t = pos.astype(F32) / length
    ang = (2.0 * math.pi) * t * band_ref[...]
    sel = sel_ref[...]
    z = jnp.where(sel == 0, t, jnp.where(sel == 1, jnp.cos(ang), jnp.where(sel == 2, -jnp.sin(ang), 0.0)))
    fr = fr_ref[...]
    h = jnp.sin(fr * (jnp.dot(z, w_in_ref[...], precision=HIGHEST, preferred_element_type=F32) + b_in_ref[...]))
    for m in range(w_mid_ref.shape[0]):
        h = jnp.sin(fr * (jnp.dot(h, w_mid_ref[m], precision=HIGHEST, preferred_element_type=F32)
                          + b_mid_ref[m]))
    o = jnp.dot(h, w_out_ref[...], precision=HIGHEST, preferred_element_type=F32) + b_out_ref[...]
    o = o * jnp.exp(-t * dl_ref[...])
    k_ref[...] = jnp.where(n == length, 0.0, o)


def _hyena_filter(length, width, fw_in, fb_in, fw_mid, fb_mid, freq, fw_out, fb_out):
    hid = fw_in.shape[1]
    n_mid = fw_mid.shape[0]
    emb = fw_in.shape[0]
    hp = LANES
    bands = jnp.linspace(1e-4, HY_BANDS - 1, HY_BANDS, dtype=F32)
    band_row = jnp.zeros((1, hp), F32).at[0, 1:1 + HY_BANDS].set(bands).at[0, 1 + HY_BANDS:emb].set(bands)
    sel_np = np.full((1, hp), 3, np.int32)
    sel_np[0, 0] = 0
    sel_np[0, 1:1 + HY_BANDS] = 1
    sel_np[0, 1 + HY_BANDS:emb] = 2
    w_in = jnp.zeros((hp, hp), F32).at[:emb, :hid].set(fw_in)
    b_in = jnp.zeros((1, hp), F32).at[0, :hid].set(fb_in)
    w_mid = jnp.zeros((n_mid, hp, hp), F32).at[:, :hid, :hid].set(fw_mid)
    b_mid = jnp.zeros((n_mid, 1, hp), F32).at[:, 0, :hid].set(fb_mid)
    fr = jnp.zeros((1, hp), F32).at[0, :hid].set(freq)
    w_out = jnp.zeros((hp, 2 * width), F32).at[:hid].set(fw_out)
    deltas = jnp.abs(jnp.linspace(HY_MIN_DECAY, HY_MAX_DECAY, width, dtype=F32)).reshape(1, width)
    n_rows = 2 * length
    tr = min(256, length)
    half = length // tr
    const2 = lambda i: (0, 0)
    const3 = lambda i: (0, 0, 0)
    side = lambda i: (0, jnp.where(i < half, 0, 1))
    return pl.pallas_call(
        functools.partial(_filter_kernel, length=length, tr=tr),
        grid=(n_rows // tr,),
        in_specs=[
            pl.BlockSpec((1, hp), const2), pl.BlockSpec((1, hp), const2),
            pl.BlockSpec((hp, hp), const2), pl.BlockSpec((1, hp), const2),
            pl.BlockSpec((n_mid, hp, hp), const3), pl.BlockSpec((n_mid, 1, hp), const3),
            pl.BlockSpec((1, hp), const2),
            pl.BlockSpec((hp, width), side), pl.BlockSpec((1, width), side),
            pl.BlockSpec((1, width), const2),
        ],
        out_specs=pl.BlockSpec((tr, width), lambda i: (i, 0)),
        out_shape=jax.ShapeDtypeStruct((n_rows, width), F32),
        compiler_params=_params("parallel"),
    )(band_row, jnp.asarray(sel_np), w_in, b_in, w_mid, b_mid, fr, w_out,
      fb_out.reshape(1, 2 * width), deltas)


def _hi_lo(a):
    a = np.asarray(a, np.float32)
    hi = a.astype(BF16)
    lo = (a - hi.astype(np.float32)).astype(BF16)
    return hi, lo


def _rot(c, s, sign):
    return [[c, sign * s], [-sign * s, c]]


@functools.lru_cache(maxsize=None)
def _direct_tables(length):
    n = 2 * length
    th = 2 * np.pi * np.outer(np.arange(n), np.arange(n)) / n
    c, s = np.cos(th), np.sin(th)
    fwd = np.block(_rot(c[:, :length], s[:, :length], 1.0))
    flt = np.concatenate([c, -s], 0)
    inv = np.block(_rot(c[:length] / n, s[:length] / n, -1.0))
    return _hi_lo(fwd), _hi_lo(flt), _hi_lo(inv)


def _fft_split(n):
    n1 = 64 if n >= 8192 else 16
    return n1, n // n1


@functools.lru_cache(maxsize=None)
def _stage_tables(n1, n2):
    n = n1 * n2
    h1 = n1 // 2
    s_ = FFT_S
    nj = n2 // s_
    k1 = np.arange(n1, dtype=np.float64)
    eye = np.eye(s_)

    def stage1(cols):
        n2v = (np.arange(nj)[:, None] * s_ + np.arange(s_)[None, :])[:, None, :, None]
        th = 2 * np.pi * (k1[None, :, None, None] * np.arange(cols)[None, None, None, :] / n1
                          + n2v * k1[None, :, None, None] / n)
        return np.cos(th), np.sin(th)

    def spread(blocks, rows, cols):
        nc, ncp = len(blocks), len(blocks[0])
        out = np.zeros((nj, nc, rows, s_, ncp, cols, s_))
        for ci in range(nc):
            for cj in range(ncp):
                out[:, ci, :, :, cj, :, :] = np.einsum("jksm,st->jksmt", blocks[ci][cj], eye)
        return out.reshape(nj, nc * rows * s_, ncp * cols * s_)

    c, s = stage1(h1)
    m1 = spread(_rot(c, s, 1.0), n1, h1)
    c, s = stage1(n1)
    m1f = spread([[c], [-s]], n1, n1)
    ph = 2 * np.pi * np.outer(np.arange(n2), np.arange(n2)) / n2
    t2 = np.block(_rot(np.cos(ph), np.sin(ph), 1.0))
    ps = 2 * np.pi * (np.outer(np.arange(n2), np.arange(n2))[None] / n2
                      + k1[:, None, None] * np.arange(n2)[None, :, None] / n)
    c, s = np.cos(ps), np.sin(ps)
    t2i = np.concatenate([np.concatenate([c, -s], 2), np.concatenate([s, c], 2)], 1)
    th = 2 * np.pi * np.outer(np.arange(h1), k1) / n1
    c, s = np.cos(th) / n, np.sin(th) / n
    blk = _rot(c, s, -1.0)
    m3 = np.zeros((2, h1, s_, 2, n1, s_))
    for ci in range(2):
        for cj in range(2):
            m3[ci, :, :, cj, :, :] = np.einsum("nk,st->nskt", blk[ci][cj], eye)
    m3 = m3.reshape(2 * h1 * s_, 2 * n1 * s_)
    return _hi_lo(m1), _hi_lo(m1f), _hi_lo(t2), _hi_lo(t2i), _hi_lo(m3)


def _split_bf16(x):
    hi = x.astype(BF16)
    return hi, (x - hi.astype(F32)).astype(BF16)


def _dot3(mh, ml, x):
    xh, xl = _split_bf16(x)
    dot = lambda a, b: jnp.dot(a, b, preferred_element_type=F32)
    return dot(mh, xh) + dot(mh, xl) + dot(ml, xh)


def _tabmm_kernel(x_ref, mh_ref, ml_ref, o_ref):
    x = x_ref[...].reshape(-1, x_ref.shape[-1])
    o_ref[...] = _dot3(mh_ref[...], ml_ref[...], x).reshape(o_ref.shape)


def _tabmm(x, x_spec, tab, tab_spec, out_shape, out_spec, grid):
    return pl.pallas_call(
        _tabmm_kernel,
        grid=grid,
        in_specs=[x_spec, tab_spec, tab_spec],
        out_specs=out_spec,
        out_shape=jax.ShapeDtypeStruct(out_shape, F32),
        compiler_params=_params(*(["parallel"] * len(grid))),
    )(x, jnp.asarray(tab[0]), jnp.asarray(tab[1]))


def _cmul_rows(s, kf_ref):
    n = s.shape[0] // 2
    sr, si = s[:n], s[n:]
    kr, ki = kf_ref[0], kf_ref[1]
    return jnp.concatenate([sr * kr - si * ki, sr * ki + si * kr], axis=0)


def _fft_mid_kernel(x_ref, kf_ref, th_ref, tl_ref, ih_ref, il_ref, o_ref):
    x = x_ref[...].reshape(-1, x_ref.shape[-1])
    y = _cmul_rows(_dot3(th_ref[...], tl_ref[...], x), kf_ref)
    o_ref[...] = _dot3(ih_ref[...], il_ref[...], y).reshape(o_ref.shape)


def _fft_s3_kernel(b_ref, mh_ref, ml_ref, v_ref, g_ref, sk_ref, z_ref):
    x = b_ref[...].reshape(-1, b_ref.shape[-1])
    y = _dot3(mh_ref[...], ml_ref[...], x).reshape(z_ref.shape)
    z_ref[...] = g_ref[...] * (y + v_ref[...] * sk_ref[...])


def _direct_conv_kernel(v_ref, g_ref, kf_ref, fh_ref, fl_ref, ih_ref, il_ref, sk_ref, z_ref):
    x = v_ref[...].reshape(-1, v_ref.shape[-1])
    y = _cmul_rows(_dot3(fh_ref[...], fl_ref[...], x), kf_ref)
    o = _dot3(ih_ref[...], il_ref[...], y).reshape(z_ref.shape)
    z_ref[...] = g_ref[...] * (o + v_ref[...] * sk_ref[...])


def _long_conv_direct(vprime, xg, kfull, skip, bsz, length, width):
    n = 2 * length
    wt = min(WT_FFT // 2, width)
    nw = width // wt
    fwd, flt, inv = _direct_tables(length)
    kf = _tabmm(kfull, pl.BlockSpec((n, wt), lambda j: (0, j)), flt,
                pl.BlockSpec((2 * n, n), lambda j: (0, 0)),
                (2, n, width), pl.BlockSpec((2, n, wt), lambda j: (0, 0, j)), (nw,))
    tok = pl.BlockSpec((2, length, wt), lambda j, p: (p, 0, j))
    const = lambda shape: pl.BlockSpec(shape, lambda j, p: (0, 0))
    z = pl.pallas_call(
        _direct_conv_kernel,
        grid=(nw, bsz // 2),
        in_specs=[tok, tok, pl.BlockSpec((2, n, wt), lambda j, p: (0, 0, j)),
                  const((2 * n, n)), const((2 * n, n)), const((n, 2 * n)), const((n, 2 * n)),
                  pl.BlockSpec((1, wt), lambda j, p: (0, j))],
        out_specs=tok,
        out_shape=jax.ShapeDtypeStruct((bsz, length, width), F32),
        compiler_params=_params("parallel", "parallel"),
    )(vprime.reshape(bsz, length, width), xg.reshape(bsz, length, width), kf,
      jnp.asarray(fwd[0]), jnp.asarray(fwd[1]), jnp.asarray(inv[0]), jnp.asarray(inv[1]),
      skip.reshape(1, width))
    return z.reshape(bsz * length, width)


def _long_conv_staged(vprime, xg, kfull, skip, bsz, length, width):
    n = 2 * length
    n1, n2 = _fft_split(n)
    h1 = n1 // 2
    s_ = FFT_S
    nj = n2 // s_
    wt = min(WT_FFT, width)
    nw = width // wt
    pairs = bsz // 2
    m1, m1f, t2, t2i, m3 = _stage_tables(n1, n2)
    r1, c1 = m1[0].shape[1:]
    spec5 = lambda p_of: pl.BlockSpec((None, 2, n1, s_, wt), lambda j, w, p: (p_of(p), 0, 0, j, w))
    tab_j = lambda cols: pl.BlockSpec((None, r1, cols), lambda j, w, p: (j, 0, 0))
    ka = _tabmm(kfull.reshape(n1, n2, width), pl.BlockSpec((n1, s_, wt), lambda j, w, p: (0, j, w)),
                m1f, tab_j(n1 * s_), (1, 2, n1, n2, width), spec5(lambda p: 0), (nj, nw, 1))
    kf = _tabmm(ka, pl.BlockSpec((None, 2, None, n2, wt), lambda k, w: (0, 0, k, 0, w)), t2,
                pl.BlockSpec((2 * n2, 2 * n2), lambda k, w: (0, 0)),
                (2, n1, n2, width), pl.BlockSpec((2, None, n2, wt), lambda k, w: (0, k, 0, w)), (n1, nw))
    v4 = vprime.reshape(bsz, h1, n2, width)
    g4 = xg.reshape(bsz, h1, n2, width)
    tok = pl.BlockSpec((2, h1, s_, wt), lambda j, w, p: (p, 0, j, w))
    a = _tabmm(v4, tok, m1, tab_j(c1), (pairs, 2, n1, n2, width), spec5(lambda p: p), (nj, nw, pairs))
    blk = pl.BlockSpec((None, 2, None, n2, wt), lambda k, w, p: (p, 0, k, 0, w))
    sq = 2 * n2
    bm = pl.pallas_call(
        _fft_mid_kernel,
        grid=(n1, nw, pairs),
        in_specs=[blk, pl.BlockSpec((2, None, n2, wt), lambda k, w, p: (0, k, 0, w)),
                  pl.BlockSpec((sq, sq), lambda k, w, p: (0, 0)), pl.BlockSpec((sq, sq), lambda k, w, p: (0, 0)),
                  pl.BlockSpec((None, sq, sq), lambda k, w, p: (k, 0, 0)),
                  pl.BlockSpec((None, sq, sq), lambda k, w, p: (k, 0, 0))],
        out_specs=blk,
        out_shape=jax.ShapeDtypeStruct((pairs, 2, n1, n2, width), F32),
        compiler_params=_params("parallel", "parallel", "parallel"),
    )(a, kf, jnp.asarray(t2[0]), jnp.asarray(t2[1]), jnp.asarray(t2i[0]), jnp.asarray(t2i[1]))
    m3_spec = pl.BlockSpec(m3[0].shape, lambda j, w, p: (0, 0))
    z = pl.pallas_call(
        _fft_s3_kernel,
        grid=(nj, nw, pairs),
        in_specs=[spec5(lambda p: p), m3_spec, m3_spec, tok, tok,
                  pl.BlockSpec((1, wt), lambda j, w, p: (0, w))],
        out_specs=tok,
        out_shape=jax.ShapeDtypeStruct((bsz, h1, n2, width), F32),
        compiler_params=_params("parallel", "parallel", "parallel"),
    )(bm, jnp.asarray(m3[0]), jnp.asarray(m3[1]), v4, g4, skip.reshape(1, width))
    return z.reshape(bsz * length, width)


def _hyena_core(vprime, xg, bsz, length, hy):
    fw_in, fb_in, fw_mid, fb_mid, freq, fw_out, fb_out, skip = hy
    width = vprime.shape[1]
    kfull = _hyena_filter(length, width, fw_in, fb_in, fw_mid, fb_mid, freq, fw_out, fb_out)
    conv = _long_conv_direct if length <= FFT_DIRECT_MAX else _long_conv_staged
    return conv(vprime, xg, kfull, skip, bsz, length, width)


def _gla_levels(cc):
    lv = []
    m = cc // 2
    while m >= GLA_DIAG:
        lv.append(m)
        m //= 2
    return lv


def _gla_level_map(cc, reverse):
    t = np.arange(cc)[:, None]
    s = np.arange(cc)[None, :]
    out = np.full((cc, cc), -1, np.int32)
    for idx, m in enumerate(_gla_levels(cc)):
        same = (t // (2 * m)) == (s // (2 * m))
        t_hi = (t % (2 * m)) >= m
        s_hi = (s % (2 * m)) >= m
        own = same & (t_hi & ~s_hi if not reverse else ~t_hi & s_hi)
        out[own] = idx
    return out


def _gla_scan_kernel(*refs, reverse, has_init, finish, scale, cc):
    refs = list(refs)
    q_ref, k_ref, v_ref, low_ref, wg_ref, bg_ref, lv_ref = refs[:7]
    pos = 7
    if has_init:
        s0_ref = refs[pos]
        pos += 1
    if finish:
        of_ref, r_ref, ng_ref = refs[pos:pos + 3]
        pos += 3
    o_ref, sf_ref, st_ref, b_ref = refs[pos:]
    c = pl.program_id(2)
    dk = q_ref.shape[1]

    @pl.when(c == 0)
    def _():
        if has_init:
            st_ref[...] = s0_ref[...]
        else:
            st_ref[...] = jnp.zeros_like(st_ref)

    logit = jnp.dot(low_ref[...], wg_ref[...], precision=HIGHEST, preferred_element_type=F32) + bg_ref[...]
    g = (jnp.minimum(logit, 0.0) - jnp.log(1.0 + jnp.exp(-jnp.abs(logit)))) / GLA_TAU

    row = lax.broadcasted_iota(jnp.int32, (cc, dk), 0)
    b = g
    k = 1
    while k < cc:
        if reverse:
            b = b + jnp.where(row < cc - k, pltpu.roll(b, cc - k, 0), 0.0)
        else:
            b = b + jnp.where(row >= k, pltpu.roll(b, k, 0), 0.0)
        k *= 2
    b_ref[...] = b
    q = q_ref[...] * scale
    kk = k_ref[...]
    vb = v_ref[...].astype(BF16)
    state = st_ref[...]

    edge = 0 if reverse else cc - 1
    b_edge = b_ref[edge:edge + 1, :]
    o = lax.dot_general((q * jnp.exp(b)).astype(BF16), state.astype(BF16),
                        (((1,), (1,)), ((), ())), preferred_element_type=F32)
    kd = (kk * jnp.exp(b_edge - b)).astype(BF16)
    upd = lax.dot_general(vb, kd, (((0,), (0,)), ((), ())), preferred_element_type=F32)
    st_ref[...] = state * jnp.exp(b_edge) + upd

    lvm = lv_ref[...]
    attn = jnp.zeros((cc, cc), F32)
    for idx, m in enumerate(_gla_levels(cc)):
        pieces = []
        for blk in range(cc // (2 * m)):
            ref_row = blk * 2 * m + (m if reverse else m - 1)
            pieces.append(jnp.broadcast_to(b_ref[ref_row:ref_row + 1, :], (2 * m, dk)))
        rr = pieces[0] if len(pieces) == 1 else jnp.concatenate(pieces, axis=0)
        later = (row & m) != 0
        qside = jnp.logical_not(later) if reverse else later
        w = jnp.exp(jnp.minimum(jnp.where(qside, b - rr, rr - b), 0.0))
        z = (jnp.where(qside, q, kk) * w).astype(BF16)
        p = lax.dot_general(z, z, (((1,), (1,)), ((), ())), preferred_element_type=F32)
        attn = jnp.where(lvm == idx, p, attn)

    dg = GLA_DIAG
    lane = lax.broadcasted_iota(jnp.int32, (dg, cc), 1)
    sub = lax.broadcasted_iota(jnp.int32, (dg, 1), 0)
    strips = []
    for blk in range(cc // dg):
        base = blk * dg
        bb = b_ref[base:base + dg, :]
        qq = q[base:base + dg, :]
        strip = jnp.zeros((dg, cc), F32)
        for j in range(dg):
            bj = b_ref[base + j:base + j + 1, :]
            kj = kk[base + j:base + j + 1, :]
            term = qq * kj * jnp.exp(jnp.minimum(bb - bj, 0.0))
            col = jnp.sum(term, axis=1, keepdims=True)
            keep = (sub <= j) if reverse else (sub >= j)
            strip = jnp.where((lane == base + j) & keep, col, strip)
        strips.append(strip)
    attn = jnp.where(lvm < 0, jnp.concatenate(strips, axis=0), attn)
    o = o + jnp.dot(attn.astype(BF16), vb, preferred_element_type=F32)

    if finish:
        o = o + of_ref[...]
        o = o * lax.rsqrt(jnp.mean(o * o, axis=-1, keepdims=True) + GLA_NORM_EPS)
        o_ref[...] = (o * ng_ref[...] * _silu(r_ref[...])).astype(o_ref.dtype)
    else:
        o_ref[...] = o

    @pl.when(c == pl.num_programs(2) - 1)
    def _():
        sf_ref[...] = st_ref[...].T


def _gla_scan(proj, low, wdir, bg, norm_g, init, o_first, bsz, length, dk, dv, direction, cc):
    reverse = direction == 1
    finish = o_first is not None
    nh = GLA_HEADS
    nc = length // cc
    t = bsz * length
    kblk = nh
    vblk = (2 * nh * dk) // dv
    rblk = (2 * nh * dk + nh * dv) // dv
    lw = low.shape[1]

    def chunk(b, c):
        return b * nc + (nc - 1 - c if reverse else c)

    head = pl.BlockSpec((cc, dv), lambda b, h, c: (chunk(b, c), h))
    in_specs = [
        pl.BlockSpec((cc, dk), lambda b, h, c: (chunk(b, c), h)),
        pl.BlockSpec((cc, dk), lambda b, h, c: (chunk(b, c), kblk + h)),
        pl.BlockSpec((cc, dv), lambda b, h, c: (chunk(b, c), vblk + h)),
        pl.BlockSpec((cc, lw), lambda b, h, c: (chunk(b, c), 0)),
        pl.BlockSpec((None, lw, dk), lambda b, h, c: (direction, 0, h)),
        pl.BlockSpec((None, 1, dk), lambda b, h, c: (direction, 0, h)),
        pl.BlockSpec((cc, cc), lambda b, h, c: (0, 0)),
    ]
    ins = [proj, proj, proj, low, wdir, bg, jnp.asarray(_gla_level_map(cc, reverse))]
    if init is not None:
        in_specs.append(pl.BlockSpec((None, None, dv, dk), lambda b, h, c: (b, h, 0, 0)))
        ins.append(init)
    if finish:
        in_specs += [head, pl.BlockSpec((cc, dv), lambda b, h, c: (chunk(b, c), rblk + h)),
                     pl.BlockSpec((1, dv), lambda b, h, c: (0, 0))]
        ins += [o_first, proj, norm_g.reshape(1, dv)]
    o, sf = pl.pallas_call(
        functools.partial(_gla_scan_kernel, reverse=reverse, has_init=init is not None, finish=finish,
                          scale=dk ** -0.5, cc=cc),
        grid=(bsz, nh, nc),
        in_specs=in_specs,
        out_specs=[head, pl.BlockSpec((None, None, dk, dv), lambda b, h, c: (b, h, 0, 0))],
        out_shape=[jax.ShapeDtypeStruct((t, nh * dv), BF16 if finish else F32),
                   jax.ShapeDtypeStruct((bsz, nh, dk, dv), F32)],
        scratch_shapes=[pltpu.VMEM((dv, dk), F32), pltpu.VMEM((cc, dk), F32)],
        compiler_params=_params("parallel", "parallel", "arbitrary"),
    )(*ins)
    return o, sf


def _transpose_kernel(x_ref, o_ref):
    o_ref[...] = x_ref[...].T


def _transpose_states(s):
    bsz, nh, dk, dv = s.shape
    return pl.pallas_call(
        _transpose_kernel,
        grid=(bsz, nh),
        in_specs=[pl.BlockSpec((None, None, dk, dv), lambda b, h: (b, h, 0, 0))],
        out_specs=pl.BlockSpec((None, None, dv, dk), lambda b, h: (b, h, 0, 0)),
        out_shape=jax.ShapeDtypeStruct((bsz, nh, dv, dk), F32),
        compiler_params=_params("parallel", "parallel"),
    )(s)


def _gla_core(u, proj, bsz, length, init, gla):
    wg1, wg2, bg, norm_g = gla
    d = u.shape[1]
    rank = wg1.shape[2]
    dk_tot = wg2.shape[2]
    dk = dk_tot // GLA_HEADS
    dv = norm_g.shape[0]
    wlow = jnp.zeros((d, LANES), F32).at[:, :rank].set(wg1[0]).at[:, rank:2 * rank].set(wg1[1])
    low = _matmul(u, wlow.astype(BF16))
    wdir = jnp.zeros((2, LANES, dk_tot), F32).at[0, :rank].set(wg2[0]).at[1, rank:2 * rank].set(wg2[1])
    cc = min(GLA_CHUNK, length)
    o, finals = None, []
    for direction in (0, 1):
        s0 = None if init is None else _transpose_states(init[:, direction])
        o, sf = _gla_scan(proj, low, wdir, bg.reshape(2, 1, dk_tot), norm_g, s0, o,
                          bsz, length, dk, dv, direction, cc)
        finals.append(sf)
    return o, finals


def kernel(x_prompt, x_sample, state_gla, c, c_ctx, ada_w, ada_b, ln_g, ln_b,
           hy_w_in, hy_conv_w, hy_conv_b, hy_fw_in, hy_fb_in, hy_fw_mid, hy_fb_mid,
           hy_freq, hy_fw_out, hy_fb_out, hy_skip, hy_w_out,
           sc_w_in, sc_conv_w, sc_w_out,
           gla_w_in, gla_wg1, gla_wg2, gla_bg, gla_norm_g, gla_w_out):
    depth, d, _ = ada_w.shape
    alpha = (2 * depth) ** 0.25
    bp, lp, _ = x_prompt.shape
    bs, ls, _ = x_sample.shape
    assert 1 + bs <= MOD_ROWS

    cond = jnp.zeros((MOD_ROWS, d), F32).at[0].set(c_ctx).at[1:1 + bs].set(c)
    mods = _modulation(cond, ada_w, ada_b)
    mods = mods.reshape(depth, MOD_ROWS, 3, 1, d).transpose(0, 2, 1, 3, 4)

    streams = [
        dict(x=x_prompt, b=bp, l=lp, period=lp, row0=0, stride=0, tt=min(TT, lp)),
        dict(x=x_sample, b=bs, l=ls, period=min(GRID_W, ls), row0=1, stride=1, tt=min(TT, ls)),
    ]
    for s in streams:
        s["u"] = _premod(s["x"], mods[0, 0], mods[0, 1], s["row0"], s["stride"], s["tt"])

    new_states = []
    for i in range(depth):
        kind = i % N_MIXERS
        j = i // N_MIXERS
        if kind == 0:
            w_in, w_out = hy_w_in[j].astype(BF16), hy_w_out[j].astype(BF16)
            hy = (hy_fw_in[j], hy_fb_in[j], hy_fw_mid[j], hy_fb_mid[j],
                  hy_freq[j], hy_fw_out[j], hy_fb_out[j], hy_skip[j])
        elif kind == 1:
            w_in, w_out = sc_w_in[j].astype(BF16), sc_w_out[j].astype(BF16)
        else:
            w_in, w_out = gla_w_in[j].astype(BF16), gla_w_out[j].astype(BF16)
            gla = (gla_wg1[j], gla_wg2[j], gla_bg[j], gla_norm_g[j])
        for si, s in enumerate(streams):
            u2 = s["u"].reshape(s["b"] * s["l"], d)
            if kind == 0:
                vprime, xg = _inproj_fused(u2, w_in, "hyena", hy_conv_w[j], hy_conv_b[j], s["period"])
                z = _hyena_core(vprime, xg, s["b"], s["l"], hy)
            elif kind == 1:
                z = _inproj_fused(u2, w_in, "sconv", sc_conv_w[j], None, s["period"])
            else:
                init = None if si == 0 else state_gla[:, j]
                z, finals = _gla_core(u2, _matmul(u2, w_in), s["b"], s["l"], init, gla)
                if si == 0:
                    new_states.append(jnp.stack(finals, axis=1).astype(x_prompt.dtype))
            h = _matmul(z, w_out)
            nxt = None if i + 1 == depth else (mods[i + 1, 0], mods[i + 1, 1])
            s["x"], s["u"] = _ln_step(s["x"], h, mods[i, 2], ln_g[i], ln_b[i], nxt, alpha,
                                      s["row0"], s["stride"], s["tt"])
    return streams[0]["x"], streams[1]["x"], jnp.stack(new_states, axis=1)
```

```python
import functools
import math

import numpy as np
import jax
import jax.numpy as jnp
from jax import lax
from jax.experimental import pallas as pl
from jax.experimental.pallas import tpu as pltpu

GRID_W = 64
N_MIXERS = 3
LN_EPS = 1e-5
HY_BANDS = 16
HY_TARGET = 1e-2
HY_FAST_PCT = 0.3
HY_SLOW_PCT = 1.5
HY_MIN_DECAY = math.log(HY_TARGET) / HY_SLOW_PCT
HY_MAX_DECAY = math.log(HY_TARGET) / HY_FAST_PCT
GLA_HEADS = 4
GLA_TAU = 16.0
GLA_NORM_EPS = 1e-6
LOG2_E = math.log2(math.e)

LANES = 128
SUBLANES = 8
MOD_ROWS = 16
VMEM_LIMIT = 56 * 1024 * 1024

TM = 1024
TN = 1024
CT_FUSED = 256
TT = 256
WT_FFT = 1024
FFT_S = SUBLANES
FFT_DIRECT_MAX = 512
GLA_CHUNK = 256
GLA_DIAG = 4

F32 = jnp.float32
BF16 = jnp.bfloat16
HIGHEST = lax.Precision.HIGHEST


def _params(*sem):
    return pltpu.CompilerParams(dimension_semantics=sem, vmem_limit_bytes=VMEM_LIMIT)


def _silu(x):
    return x / (1.0 + jnp.exp(-x))


def _mod_kernel(c_ref, w_ref, b_ref, o_ref):
    s = _silu(c_ref[...])
    o_ref[...] = jnp.dot(s, w_ref[...], precision=HIGHEST, preferred_element_type=F32) + b_ref[...]


def _modulation(cond, ada_w, ada_b):
    depth, d, n = ada_w.shape
    tn = 512
    return pl.pallas_call(
        _mod_kernel,
        grid=(depth, n // tn),
        in_specs=[
            pl.BlockSpec((MOD_ROWS, d), lambda i, j: (0, 0)),
            pl.BlockSpec((None, d, tn), lambda i, j: (i, 0, j)),
            pl.BlockSpec((None, 1, tn), lambda i, j: (i, 0, j)),
        ],
        out_specs=pl.BlockSpec((None, MOD_ROWS, tn), lambda i, j: (i, 0, j)),
        out_shape=jax.ShapeDtypeStruct((depth, MOD_ROWS, n), F32),
        compiler_params=_params("parallel", "parallel"),
    )(cond, ada_w, ada_b.reshape(depth, 1, n))


def _premod_kernel(x_ref, sh_ref, sc_ref, u_ref):
    u_ref[...] = (x_ref[...] * (1.0 + sc_ref[...]) + sh_ref[...]).astype(BF16)


def _row_spec(d, row0, stride):
    return pl.BlockSpec((None, 1, d), lambda b, t: (row0 + stride * b, 0, 0))


def _premod(x, shift, scale, row0, stride, tt):
    bsz, length, d = x.shape
    tok = pl.BlockSpec((None, tt, d), lambda b, t: (b, t, 0))
    return pl.pallas_call(
        _premod_kernel,
        grid=(bsz, length // tt),
        in_specs=[tok, _row_spec(d, row0, stride), _row_spec(d, row0, stride)],
        out_specs=tok,
        out_shape=jax.ShapeDtypeStruct(x.shape, BF16),
        compiler_params=_params("parallel", "parallel"),
    )(x, shift, scale)


def _mm_kernel(a_ref, b_ref, o_ref):
    o_ref[...] = jnp.dot(a_ref[...].astype(BF16), b_ref[...], preferred_element_type=F32).astype(o_ref.dtype)


def _matmul(a, b, out_dtype=F32):
    m, k = a.shape
    n = b.shape[1]
    tm = min(TM if a.dtype == BF16 else TM // 2, m)
    tn = min(TN, n)
    assert m % tm == 0 and n % tn == 0
    return pl.pallas_call(
        _mm_kernel,
        grid=(n // tn, m // tm),
        in_specs=[
            pl.BlockSpec((tm, k), lambda j, i: (i, 0)),
            pl.BlockSpec((k, tn), lambda j, i: (0, j)),
        ],
        out_specs=pl.BlockSpec((tm, tn), lambda j, i: (i, j)),
        out_shape=jax.ShapeDtypeStruct((m, n), out_dtype),
        compiler_params=_params("parallel", "parallel"),
    )(a, b)


def _conv3_rows(x, w_ref, period):
    rows = x.shape[0]
    pos = lax.broadcasted_iota(jnp.int32, x.shape, 0) % period
    prev = jnp.where(pos == 0, 0.0, pltpu.roll(x, 1, 0))
    nxt = jnp.where(pos == period - 1, 0.0, pltpu.roll(x, rows - 1, 0))
    return prev * w_ref[0:1, :] + x * w_ref[1:2, :] + nxt * w_ref[2:3, :]


def _inproj_sconv_kernel(a_ref, wb_ref, wc_ref, wx_ref, wg_ref, cw_ref, z_ref, *, period):
    a = a_ref[...]
    dot = lambda w_ref: jnp.dot(a, w_ref[...], preferred_element_type=F32)
    y = dot(wb_ref) * _conv3_rows(dot(wc_ref) * dot(wx_ref), cw_ref, period)
    z_ref[...] = (y * _silu(dot(wg_ref))).astype(BF16)


def _inproj_hyena_kernel(a_ref, w0_ref, w1_ref, w2_ref, wg_ref, c0_ref, c1_ref, c2_ref,
                         b0_ref, b1_ref, b2_ref, v_ref, xg_ref, *, period):
    a = a_ref[...]
    dot = lambda w_ref: jnp.dot(a, w_ref[...], preferred_element_type=F32)
    x1 = _conv3_rows(dot(w1_ref), c1_ref, period) + b1_ref[...]
    v = _conv3_rows(dot(w2_ref), c2_ref, period) + b2_ref[...]
    v_ref[...] = v * x1
    x0 = _conv3_rows(dot(w0_ref), c0_ref, period) + b0_ref[...]
    xg_ref[...] = x0 * _silu(dot(wg_ref))


def _inproj_fused(u, w_in, kind, conv_w, conv_b, period):
    t, d = u.shape
    w = w_in.shape[1] // 4
    ct = min(CT_FUSED, w)
    tm = min(TM, t)
    nb = w // ct
    assert t % tm == 0 and tm % period == 0

    def col(rows, k):
        return pl.BlockSpec((rows, ct), lambda j, i, k=k: (0, k * nb + j))

    a_spec = pl.BlockSpec((tm, d), lambda j, i: (i, 0))
    out = pl.BlockSpec((tm, ct), lambda j, i: (i, j))
    w_specs = [col(d, k) for k in range(4)]
    if kind == "sconv":
        return pl.pallas_call(
            functools.partial(_inproj_sconv_kernel, period=period),
            grid=(nb, t // tm),
            in_specs=[a_spec] + w_specs + [col(3, 0)],
            out_specs=out,
            out_shape=jax.ShapeDtypeStruct((t, w), BF16),
            compiler_params=_params("parallel", "parallel"),
        )(u, w_in, w_in, w_in, w_in, conv_w)
    cb = conv_b.reshape(1, 3 * w)
    return pl.pallas_call(
        functools.partial(_inproj_hyena_kernel, period=period),
        grid=(nb, t // tm),
        in_specs=[a_spec] + w_specs + [col(3, k) for k in range(3)] + [col(1, k) for k in range(3)],
        out_specs=[out, out],
        out_shape=[jax.ShapeDtypeStruct((t, w), F32)] * 2,
        compiler_params=_params("parallel", "parallel"),
    )(u, w_in, w_in, w_in, w_in, conv_w, conv_w, conv_w, cb, cb, cb)


def _ln_kernel(*refs, alpha, with_next):
    if with_next:
        x_ref, h_ref, gt_ref, g_ref, b_ref, sh_ref, sc_ref, xo_ref, uo_ref = refs
    else:
        x_ref, h_ref, gt_ref, g_ref, b_ref, xo_ref = refs
    y = alpha * x_ref[...] + gt_ref[...] * h_ref[...]
    mu = jnp.mean(y, axis=-1, keepdims=True)
    dlt = y - mu
    var = jnp.mean(dlt * dlt, axis=-1, keepdims=True)
    xn = dlt * lax.rsqrt(var + LN_EPS) * g_ref[...] + b_ref[...]
    xo_ref[...] = xn
    if with_next:
        uo_ref[...] = (xn * (1.0 + sc_ref[...]) + sh_ref[...]).astype(BF16)


def _ln_step(x, h, gate, ln_g, ln_b, nxt, alpha, row0, stride, tt):
    bsz, length, d = x.shape
    tok = pl.BlockSpec((None, tt, d), lambda b, t: (b, t, 0))
    row = _row_spec(d, row0, stride)
    vec = pl.BlockSpec((1, d), lambda b, t: (0, 0))
    ins = [x, h.reshape(x.shape), gate, ln_g.reshape(1, d), ln_b.reshape(1, d)]
    specs = [tok, tok, row, vec, vec]
    out_shape = [jax.ShapeDtypeStruct(x.shape, F32)]
    out_specs = [tok]
    if nxt is not None:
        ins += [nxt[0], nxt[1]]
        specs += [row, row]
        out_shape.append(jax.ShapeDtypeStruct(x.shape, BF16))
        out_specs.append(tok)
    res = pl.pallas_call(
        functools.partial(_ln_kernel, alpha=alpha, with_next=nxt is not None),
        grid=(bsz, length // tt),
        in_specs=specs,
        out_specs=out_specs,
        out_shape=out_shape,
        compiler_params=_params("parallel", "parallel"),
    )(*ins)
    return (res[0], res[1]) if nxt is not None else (res[0], None)


def _filter_kernel(band_ref, sel_ref, w_in_ref, b_in_ref, w_mid_ref, b_mid_ref, fr_ref,
                   w_out_ref, b_out_ref, dl_ref, k_ref, *, length, tr):
    n = pl.program_id(0) * tr + lax.broadcasted_iota(jnp.int32, (tr, 1), 0)
    pos = jnp.where(n < length, n, 2 * length - n)
    t = pos.astype(F32) / length
    ang = (2.0 * math.pi) * t * band_ref[...]
    sel = sel_ref[...]
    z = jnp.where(sel == 0, t, jnp.where(sel == 1, jnp.cos(ang), jnp.where(sel == 2, -jnp.sin(ang), 0.0)))
    fr = fr_ref[...]
    h = jnp.sin(fr * (jnp.dot(z, w_in_ref[...], precision=HIGHEST, preferred_element_type=F32) + b_in_ref[...]))
    for m in range(w_mid_ref.shape[0]):
        h = jnp.sin(fr * (jnp.dot(h, w_mid_ref[m], precision=HIGHEST, preferred_element_type=F32)
                          + b_mid_ref[m]))
    o = jnp.dot(h, w_out_ref[...], precision=HIGHEST, preferred_element_type=F32) + b_out_ref[...]
    o = o * jnp.exp(-t * dl_ref[...])
    k_ref[...] = jnp.where(n == length, 0.0, o)


def _hyena_filter(length, width, fw_in, fb_in, fw_mid, fb_mid, freq, fw_out, fb_out):
    hid = fw_in.shape[1]
    n_mid = fw_mid.shape[0]
    emb = fw_in.shape[0]
    hp = LANES
    bands = jnp.linspace(1e-4, HY_BANDS - 1, HY_BANDS, dtype=F32)
    band_row = jnp.zeros((1, hp), F32).at[0, 1:1 + HY_BANDS].set(bands).at[0, 1 + HY_BANDS:emb].set(bands)
    sel_np = np.full((1, hp), 3, np.int32)
    sel_np[0, 0] = 0
    sel_np[0, 1:1 + HY_BANDS] = 1
    sel_np[0, 1 + HY_BANDS:emb] = 2
    w_in = jnp.zeros((hp, hp), F32).at[:emb, :hid].set(fw_in)
    b_in = jnp.zeros((1, hp), F32).at[0, :hid].set(fb_in)
    w_mid = jnp.zeros((n_mid, hp, hp), F32).at[:, :hid, :hid].set(fw_mid)
    b_mid = jnp.zeros((n_mid, 1, hp), F32).at[:, 0, :hid].set(fb_mid)
    fr = jnp.zeros((1, hp), F32).at[0, :hid].set(freq)
    w_out = jnp.zeros((hp, 2 * width), F32).at[:hid].set(fw_out)
    deltas = jnp.abs(jnp.linspace(HY_MIN_DECAY, HY_MAX_DECAY, width, dtype=F32)).reshape(1, width)
    n_rows = 2 * length
    tr = min(256, length)
    half = length // tr
    const2 = lambda i: (0, 0)
    const3 = lambda i: (0, 0, 0)
    side = lambda i: (0, jnp.where(i < half, 0, 1))
    return pl.pallas_call(
        functools.partial(_filter_kernel, length=length, tr=tr),
        grid=(n_rows // tr,),
        in_specs=[
            pl.BlockSpec((1, hp), const2), pl.BlockSpec((1, hp), const2),
            pl.BlockSpec((hp, hp), const2), pl.BlockSpec((1, hp), const2),
            pl.BlockSpec((n_mid, hp, hp), const3), pl.BlockSpec((n_mid, 1, hp), const3),
            pl.BlockSpec((1, hp), const2),
            pl.BlockSpec((hp, width), side), pl.BlockSpec((1, width), side),
            pl.BlockSpec((1, width), const2),
        ],
        out_specs=pl.BlockSpec((tr, width), lambda i: (i, 0)),
        out_shape=jax.ShapeDtypeStruct((n_rows, width), F32),
        compiler_params=_params("parallel"),
    )(band_row, jnp.asarray(sel_np), w_in, b_in, w_mid, b_mid, fr, w_out,
      fb_out.reshape(1, 2 * width), deltas)


def _hi_lo(a):
    a = np.asarray(a, np.float32)
    hi = a.astype(BF16)
    lo = (a - hi.astype(np.float32)).astype(BF16)
    return hi, lo


def _rot(c, s, sign):
    return [[c, sign * s], [-sign * s, c]]


@functools.lru_cache(maxsize=None)
def _direct_tables(length):
    n = 2 * length
    th = 2 * np.pi * np.outer(np.arange(n), np.arange(n)) / n
    c, s = np.cos(th), np.sin(th)
    fwd = np.block(_rot(c[:, :length], s[:, :length], 1.0))
    flt = np.concatenate([c, -s], 0)
    inv = np.block(_rot(c[:length] / n, s[:length] / n, -1.0))
    return _hi_lo(fwd), _hi_lo(flt), _hi_lo(inv)


def _fft_split(n):
    n1 = 64 if n >= 8192 else 16
    return n1, n // n1


@functools.lru_cache(maxsize=None)
def _stage_tables(n1, n2):
    n = n1 * n2
    h1 = n1 // 2
    s_ = FFT_S
    nj = n2 // s_
    k1 = np.arange(n1, dtype=np.float64)
    eye = np.eye(s_)

    def stage1(cols):
        n2v = (np.arange(nj)[:, None] * s_ + np.arange(s_)[None, :])[:, None, :, None]
        th = 2 * np.pi * (k1[None, :, None, None] * np.arange(cols)[None, None, None, :] / n1
                          + n2v * k1[None, :, None, None] / n)
        return np.cos(th), np.sin(th)

    def spread(blocks, rows, cols):
        nc, ncp = len(blocks), len(blocks[0])
        out = np.zeros((nj, nc, rows, s_, ncp, cols, s_))
        for ci in range(nc):
            for cj in range(ncp):
                out[:, ci, :, :, cj, :, :] = np.einsum("jksm,st->jksmt", blocks[ci][cj], eye)
        return out.reshape(nj, nc * rows * s_, ncp * cols * s_)

    c, s = stage1(h1)
    m1 = spread(_rot(c, s, 1.0), n1, h1)
    c, s = stage1(n1)
    m1f = spread([[c], [-s]], n1, n1)
    ph = 2 * np.pi * np.outer(np.arange(n2), np.arange(n2)) / n2
    t2 = np.block(_rot(np.cos(ph), np.sin(ph), 1.0))
    ps = 2 * np.pi * (np.outer(np.arange(n2), np.arange(n2))[None] / n2
                      + k1[:, None, None] * np.arange(n2)[None, :, None] / n)
    c, s = np.cos(ps), np.sin(ps)
    t2i = np.concatenate([np.concatenate([c, -s], 2), np.concatenate([s, c], 2)], 1)
    th = 2 * np.pi * np.outer(np.arange(h1), k1) / n1
    c, s = np.cos(th) / n, np.sin(th) / n
    blk = _rot(c, s, -1.0)
    m3 = np.zeros((2, h1, s_, 2, n1, s_))
    for ci in range(2):
        for cj in range(2):
            m3[ci, :, :, cj, :, :] = np.einsum("nk,st->nskt", blk[ci][cj], eye)
    m3 = m3.reshape(2 * h1 * s_, 2 * n1 * s_)
    return _hi_lo(m1), _hi_lo(m1f), _hi_lo(t2), _hi_lo(t2i), _hi_lo(m3)


def _split_bf16(x):
    hi = x.astype(BF16)
    return hi, (x - hi.astype(F32)).astype(BF16)


def _dot3(mh, ml, x):
    xh, xl = _split_bf16(x)
    dot = lambda a, b: jnp.dot(a, b, preferred_element_type=F32)
    return dot(mh, xh) + dot(mh, xl) + dot(ml, xh)


def _dot1(mh, x):
    return jnp.dot(mh, x.astype(BF16), preferred_element_type=F32)


def _tabmm_kernel(*refs, passes):
    x_ref, o_ref = refs[0], refs[-1]
    x = x_ref[...].reshape(-1, x_ref.shape[-1])
    y = _dot1(refs[1][...], x) if passes == 1 else _dot3(refs[1][...], refs[2][...], x)
    o_ref[...] = y.reshape(o_ref.shape)


def _tabmm(x, x_spec, tab, tab_spec, out_shape, out_spec, grid, passes):
    tabs = [jnp.asarray(tab[0])] + ([jnp.asarray(tab[1])] if passes == 3 else [])
    return pl.pallas_call(
        functools.partial(_tabmm_kernel, passes=passes),
        grid=grid,
        in_specs=[x_spec] + [tab_spec] * len(tabs),
        out_specs=out_spec,
        out_shape=jax.ShapeDtypeStruct(out_shape, F32),
        compiler_params=_params(*(["parallel"] * len(grid))),
    )(x, *tabs)


def _cmul_rows(s, kf_ref):
    n = s.shape[0] // 2
    sr, si = s[:n], s[n:]
    kr, ki = kf_ref[0], kf_ref[1]
    return jnp.concatenate([sr * kr - si * ki, sr * ki + si * kr], axis=0)


def _fft_mid_kernel(x_ref, kf_ref, t_ref, i_ref, o_ref):
    x = x_ref[...].reshape(-1, x_ref.shape[-1])
    y = _cmul_rows(_dot1(t_ref[...], x), kf_ref)
    o_ref[...] = _dot1(i_ref[...], y).reshape(o_ref.shape)


def _fft_s3_kernel(b_ref, m_ref, v_ref, g_ref, sk_ref, z_ref):
    x = b_ref[...].reshape(-1, b_ref.shape[-1])
    y = _dot1(m_ref[...], x).reshape(z_ref.shape)
    z_ref[...] = g_ref[...] * (y + v_ref[...] * sk_ref[...])


def _direct_conv_kernel(v_ref, g_ref, kf_ref, f_ref, i_ref, sk_ref, z_ref):
    x = v_ref[...].reshape(-1, v_ref.shape[-1])
    y = _cmul_rows(_dot1(f_ref[...], x), kf_ref)
    o = _dot1(i_ref[...], y).reshape(z_ref.shape)
    z_ref[...] = g_ref[...] * (o + v_ref[...] * sk_ref[...])


def _long_conv_direct(vprime, xg, kfull, skip, bsz, length, width):
    n = 2 * length
    wt = min(WT_FFT // 2, width)
    nw = width // wt
    fwd, flt, inv = _direct_tables(length)
    kf = _tabmm(kfull, pl.BlockSpec((n, wt), lambda j: (0, j)), flt,
                pl.BlockSpec((2 * n, n), lambda j: (0, 0)),
                (2, n, width), pl.BlockSpec((2, n, wt), lambda j: (0, 0, j)), (nw,), 3)
    tok = pl.BlockSpec((2, length, wt), lambda j, p: (p, 0, j))
    const = lambda shape: pl.BlockSpec(shape, lambda j, p: (0, 0))
    z = pl.pallas_call(
        _direct_conv_kernel,
        grid=(nw, bsz // 2),
        in_specs=[tok, tok, pl.BlockSpec((2, n, wt), lambda j, p: (0, 0, j)),
                  const((2 * n, n)), const((n, 2 * n)),
                  pl.BlockSpec((1, wt), lambda j, p: (0, j))],
        out_specs=tok,
        out_shape=jax.ShapeDtypeStruct((bsz, length, width), F32),
        compiler_params=_params("parallel", "parallel"),
    )(vprime.reshape(bsz, length, width), xg.reshape(bsz, length, width), kf,
      jnp.asarray(fwd[0]), jnp.asarray(inv[0]), skip.reshape(1, width))
    return z.reshape(bsz * length, width)


def _long_conv_staged(vprime, xg, kfull, skip, bsz, length, width):
    n = 2 * length
    n1, n2 = _fft_split(n)
    h1 = n1 // 2
    s_ = FFT_S
    nj = n2 // s_
    wt = min(WT_FFT, width)
    nw = width // wt
    pairs = bsz // 2
    m1, m1f, t2, t2i, m3 = _stage_tables(n1, n2)
    r1, c1 = m1[0].shape[1:]
    spec5 = lambda p_of: pl.BlockSpec((None, 2, n1, s_, wt), lambda j, w, p: (p_of(p), 0, 0, j, w))
    tab_j = lambda cols: pl.BlockSpec((None, r1, cols), lambda j, w, p: (j, 0, 0))
    ka = _tabmm(kfull.reshape(n1, n2, width), pl.BlockSpec((n1, s_, wt), lambda j, w, p: (0, j, w)),
                m1f, tab_j(n1 * s_), (1, 2, n1, n2, width), spec5(lambda p: 0), (nj, nw, 1), 3)
    kf = _tabmm(ka, pl.BlockSpec((None, 2, None, n2, wt), lambda k, w: (0, 0, k, 0, w)), t2,
                pl.BlockSpec((2 * n2, 2 * n2), lambda k, w: (0, 0)),
                (2, n1, n2, width), pl.BlockSpec((2, None, n2, wt), lambda k, w: (0, k, 0, w)), (n1, nw), 3)
    v4 = vprime.reshape(bsz, h1, n2, width)
    g4 = xg.reshape(bsz, h1, n2, width)
    tok = pl.BlockSpec((2, h1, s_, wt), lambda j, w, p: (p, 0, j, w))
    a = _tabmm(v4, tok, m1, tab_j(c1), (pairs, 2, n1, n2, width), spec5(lambda p: p), (nj, nw, pairs), 1)
    blk = pl.BlockSpec((None, 2, None, n2, wt), lambda k, w, p: (p, 0, k, 0, w))
    sq = 2 * n2
    bm = pl.pallas_call(
        _fft_mid_kernel,
        grid=(n1, nw, pairs),
        in_specs=[blk, pl.BlockSpec((2, None, n2, wt), lambda k, w, p: (0, k, 0, w)),
                  pl.BlockSpec((sq, sq), lambda k, w, p: (0, 0)),
                  pl.BlockSpec((None, sq, sq), lambda k, w, p: (k, 0, 0))],
        out_specs=blk,
        out_shape=jax.ShapeDtypeStruct((pairs, 2, n1, n2, width), F32),
        compiler_params=_params("parallel", "parallel", "parallel"),
    )(a, kf, jnp.asarray(t2[0]), jnp.asarray(t2i[0]))
    m3_spec = pl.BlockSpec(m3[0].shape, lambda j, w, p: (0, 0))
    z = pl.pallas_call(
        _fft_s3_kernel,
        grid=(nj, nw, pairs),
        in_specs=[spec5(lambda p: p), m3_spec, tok, tok,
                  pl.BlockSpec((1, wt), lambda j, w, p: (0, w))],
        out_specs=tok,
        out_shape=jax.ShapeDtypeStruct((bsz, h1, n2, width), F32),
        compiler_params=_params("parallel", "parallel", "parallel"),
    )(bm, jnp.asarray(m3[0]), v4, g4, skip.reshape(1, width))
    return z.reshape(bsz * length, width)


def _hyena_core(vprime, xg, bsz, length, hy):
    fw_in, fb_in, fw_mid, fb_mid, freq, fw_out, fb_out, skip = hy
    width = vprime.shape[1]
    kfull = _hyena_filter(length, width, fw_in, fb_in, fw_mid, fb_mid, freq, fw_out, fb_out)
    conv = _long_conv_direct if length <= FFT_DIRECT_MAX else _long_conv_staged
    return conv(vprime, xg, kfull, skip, bsz, length, width)


def _gla_levels(cc):
    lv = []
    m = cc // 2
    while m >= GLA_DIAG:
        lv.append(m)
        m //= 2
    return lv


def _gla_level_map(cc, reverse):
    t = np.arange(cc)[:, None]
    s = np.arange(cc)[None, :]
    out = np.full((cc, cc), -1, np.int32)
    for idx, m in enumerate(_gla_levels(cc)):
        same = (t // (2 * m)) == (s // (2 * m))
        t_hi = (t % (2 * m)) >= m
        s_hi = (s % (2 * m)) >= m
        own = same & (t_hi & ~s_hi if not reverse else ~t_hi & s_hi)
        out[own] = idx
    return out


def _gla_scan_kernel(*refs, reverse, has_init, keep_state, finish, scale, cc):
    refs = list(refs)
    q_ref, k_ref, v_ref, low_ref, wg_ref, bg_ref, lv_ref = refs[:7]
    pos = 7
    if has_init:
        s0_ref = refs[pos]
        pos += 1
    if finish:
        of_ref, r_ref, ng_ref = refs[pos:pos + 3]
        pos += 3
    if keep_state and reverse:
        pos += 1
    o_ref = refs[pos]
    sf_ref = refs[pos + 1] if keep_state else None
    st_ref, b_ref = refs[-2:]
    c = pl.program_id(2)
    dk = q_ref.shape[1]

    @pl.when(c == 0)
    def _():
        if has_init:
            st_ref[...] = s0_ref[...].T
        else:
            st_ref[...] = jnp.zeros_like(st_ref)

    logit = jnp.dot(low_ref[...].astype(BF16), wg_ref[...], preferred_element_type=F32) + bg_ref[...]
    g = (jnp.minimum(logit, 0.0) - jnp.log(1.0 + jnp.exp(-jnp.abs(logit)))) * (LOG2_E / GLA_TAU)

    row = lax.broadcasted_iota(jnp.int32, (cc, dk), 0)
    b = g
    k = 1
    while k < cc:
        if reverse:
            b = b + jnp.where(row < cc - k, pltpu.roll(b, cc - k, 0), 0.0)
        else:
            b = b + jnp.where(row >= k, pltpu.roll(b, k, 0), 0.0)
        k *= 2
    b_ref[...] = b
    q = q_ref[...] * scale
    kk = k_ref[...]
    vb = v_ref[...].astype(BF16)
    state = st_ref[...]

    edge = 0 if reverse else cc - 1
    b_edge = b_ref[edge:edge + 1, :]
    o = lax.dot_general((q * jnp.exp2(b)).astype(BF16), state.astype(BF16),
                        (((1,), (1,)), ((), ())), preferred_element_type=F32)
    kd = (kk * jnp.exp2(b_edge - b)).astype(BF16)
    upd = lax.dot_general(vb, kd, (((0,), (0,)), ((), ())), preferred_element_type=F32)
    st_ref[...] = state * jnp.exp2(b_edge) + upd

    lvm = lv_ref[...]
    attn = jnp.zeros((cc, cc), F32)
    for idx, m in enumerate(_gla_levels(cc)):
        pieces = []
        for blk in range(cc // (2 * m)):
            ref_row = blk * 2 * m + (m if reverse else m - 1)
            pieces.append(jnp.broadcast_to(b_ref[ref_row:ref_row + 1, :], (2 * m, dk)))
        rr = pieces[0] if len(pieces) == 1 else jnp.concatenate(pieces, axis=0)
        later = (row & m) != 0
        qside = jnp.logical_not(later) if reverse else later
        w = jnp.exp2(jnp.minimum(jnp.where(qside, b - rr, rr - b), 0.0))
        z = (jnp.where(qside, q, kk) * w).astype(BF16)
        p = lax.dot_general(z, z, (((1,), (1,)), ((), ())), preferred_element_type=F32)
        attn = jnp.where(lvm == idx, p, attn)

    ti = lax.broadcasted_iota(jnp.int32, (cc, cc), 0)
    si = lax.broadcasted_iota(jnp.int32, (cc, cc), 1)
    tin = ti & (GLA_DIAG - 1)
    diag = jnp.zeros((cc, cc), F32)
    for dlt in range(GLA_DIAG):
        if dlt == 0:
            term = q * kk
        else:
            shift = cc - dlt if reverse else dlt
            decay = jnp.exp2(jnp.minimum(b - pltpu.roll(b, shift, 0), 0.0))
            term = q * pltpu.roll(kk, shift, 0) * decay
        col = jnp.sum(term, axis=1, keepdims=True)
        if reverse:
            own = (si == ti + dlt) & (tin < GLA_DIAG - dlt)
        else:
            own = (si == ti - dlt) & (tin >= dlt)
        diag = jnp.where(own, col, diag)
    attn = jnp.where(lvm < 0, diag, attn)
    o = o + jnp.dot(attn.astype(BF16), vb, preferred_element_type=F32)

    if finish:
        o = o + of_ref[...]
        o = o * lax.rsqrt(jnp.mean(o * o, axis=-1, keepdims=True) + GLA_NORM_EPS)
        o_ref[...] = (o * ng_ref[...] * _silu(r_ref[...])).astype(o_ref.dtype)
    else:
        o_ref[...] = o

    if keep_state:
        @pl.when(c == pl.num_programs(2) - 1)
        def _():
            sf_ref[...] = st_ref[...].T


def _gla_scan(proj, low, wdir, bg, norm_g, init, o_first, states, bsz, length, dk, dv, direction, cc):
    reverse = direction == 1
    finish = o_first is not None
    keep_state = init is None
    nh = GLA_HEADS
    nc = length // cc
    t = bsz * length
    kblk = nh
    vblk = (2 * nh * dk) // dv
    rblk = (2 * nh * dk + nh * dv) // dv
    lw = low.shape[1]

    def chunk(b, c):
        return b * nc + (nc - 1 - c if reverse else c)

    head = pl.BlockSpec((cc, dv), lambda b, h, c: (chunk(b, c), h))
    in_specs = [
        pl.BlockSpec((cc, dk), lambda b, h, c: (chunk(b, c), h)),
        pl.BlockSpec((cc, dk), lambda b, h, c: (chunk(b, c), kblk + h)),
        pl.BlockSpec((cc, dv), lambda b, h, c: (chunk(b, c), vblk + h)),
        pl.BlockSpec((cc, lw), lambda b, h, c: (chunk(b, c), 0)),
        pl.BlockSpec((None, lw, dk), lambda b, h, c: (direction, 0, h)),
        pl.BlockSpec((None, 1, dk), lambda b, h, c: (direction, 0, h)),
        pl.BlockSpec((cc, cc), lambda b, h, c: (0, 0)),
    ]
    ins = [proj, proj, proj, low, wdir, bg, jnp.asarray(_gla_level_map(cc, reverse))]
    if init is not None:
        state, layer = init
        in_specs.append(pl.BlockSpec((None, None, None, None, dk, dv),
                                     lambda b, h, c: (b, layer, direction, h, 0, 0)))
        ins.append(state)
    if finish:
        in_specs += [head, pl.BlockSpec((cc, dv), lambda b, h, c: (chunk(b, c), rblk + h)),
                     pl.BlockSpec((1, dv), lambda b, h, c: (0, 0))]
        ins += [o_first, proj, norm_g.reshape(1, dv)]
    out_specs = [head]
    out_shape = [jax.ShapeDtypeStruct((t, nh * dv), BF16 if finish else F32)]
    aliases = {}
    if keep_state:
        out_specs.append(pl.BlockSpec((None, None, None, dk, dv), lambda b, h, c: (b, direction, h, 0, 0)))
        out_shape.append(jax.ShapeDtypeStruct((bsz, 2, nh, dk, dv), F32))
        if states is not None:
            in_specs.append(pl.BlockSpec(memory_space=pl.ANY))
            ins.append(states)
            aliases = {len(ins) - 1: 1}
    res = pl.pallas_call(
        functools.partial(_gla_scan_kernel, reverse=reverse, has_init=init is not None,
                          keep_state=keep_state, finish=finish, scale=dk ** -0.5, cc=cc),
        grid=(bsz, nh, nc),
        in_specs=in_specs,
        out_specs=out_specs,
        out_shape=out_shape,
        input_output_aliases=aliases,
        scratch_shapes=[pltpu.VMEM((dv, dk), F32), pltpu.VMEM((cc, dk), F32)],
        compiler_params=_params("parallel", "parallel", "arbitrary"),
    )(*ins)
    return (res[0], res[1]) if keep_state else (res[0], None)


def _gla_core(u, proj, bsz, length, init, gla):
    wg1, wg2, bg, norm_g = gla
    d = u.shape[1]
    rank = wg1.shape[2]
    dk_tot = wg2.shape[2]
    dk = dk_tot // GLA_HEADS
    dv = norm_g.shape[0]
    wlow = jnp.zeros((d, LANES), F32).at[:, :rank].set(wg1[0]).at[:, rank:2 * rank].set(wg1[1])
    low = _matmul(u, wlow.astype(BF16))
    wdir = jnp.zeros((2, LANES, dk_tot), F32).at[0, :rank].set(wg2[0]).at[1, rank:2 * rank].set(wg2[1])
    wdir = wdir.astype(BF16)
    cc = min(GLA_CHUNK, length)
    o, states = None, None
    for direction in (0, 1):
        o, states = _gla_scan(proj, low, wdir, bg.reshape(2, 1, dk_tot), norm_g, init, o, states,
                              bsz, length, dk, dv, direction, cc)
    return o, states


def kernel(x_prompt, x_sample, state_gla, c, c_ctx, ada_w, ada_b, ln_g, ln_b,
           hy_w_in, hy_conv_w, hy_conv_b, hy_fw_in, hy_fb_in, hy_fw_mid, hy_fb_mid,
           hy_freq, hy_fw_out, hy_fb_out, hy_skip, hy_w_out,
           sc_w_in, sc_conv_w, sc_w_out,
           gla_w_in, gla_wg1, gla_wg2, gla_bg, gla_norm_g, gla_w_out):
    depth, d, _ = ada_w.shape
    alpha = (2 * depth) ** 0.25
    bp, lp, _ = x_prompt.shape
    bs, ls, _ = x_sample.shape
    assert 1 + bs <= MOD_ROWS

    cond = jnp.zeros((MOD_ROWS, d), F32).at[0].set(c_ctx).at[1:1 + bs].set(c)
    mods = _modulation(cond, ada_w, ada_b)
    mods = mods.reshape(depth, MOD_ROWS, 3, 1, d).transpose(0, 2, 1, 3, 4)

    streams = [
        dict(x=x_prompt, b=bp, l=lp, period=lp, row0=0, stride=0, tt=min(TT, lp)),
        dict(x=x_sample, b=bs, l=ls, period=min(GRID_W, ls), row0=1, stride=1, tt=min(TT, ls)),
    ]
    for s in streams:
        s["u"] = _premod(s["x"], mods[0, 0], mods[0, 1], s["row0"], s["stride"], s["tt"])

    new_states = []
    for i in range(depth):
        kind = i % N_MIXERS
        j = i // N_MIXERS
        if kind == 0:
            w_in, w_out = hy_w_in[j].astype(BF16), hy_w_out[j].astype(BF16)
            hy = (hy_fw_in[j], hy_fb_in[j], hy_fw_mid[j], hy_fb_mid[j],
                  hy_freq[j], hy_fw_out[j], hy_fb_out[j], hy_skip[j])
        elif kind == 1:
            w_in, w_out = sc_w_in[j].astype(BF16), sc_w_out[j].astype(BF16)
        else:
            w_in, w_out = gla_w_in[j].astype(BF16), gla_w_out[j].astype(BF16)
            gla = (gla_wg1[j], gla_wg2[j], gla_bg[j], gla_norm_g[j])
        for si, s in enumerate(streams):
            u2 = s["u"].reshape(s["b"] * s["l"], d)
            if kind == 0:
                vprime, xg = _inproj_fused(u2, w_in, "hyena", hy_conv_w[j], hy_conv_b[j], s["period"])
                z = _hyena_core(vprime, xg, s["b"], s["l"], hy)
            elif kind == 1:
                z = _inproj_fused(u2, w_in, "sconv", sc_conv_w[j], None, s["period"])
            else:
                init = None if si == 0 else (state_gla, j)
                z, finals = _gla_core(u2, _matmul(u2, w_in), s["b"], s["l"], init, gla)
                if si == 0:
                    new_states.append(finals.astype(x_prompt.dtype))
            h = _matmul(z, w_out, BF16)
            nxt = None if i + 1 == depth else (mods[i + 1, 0], mods[i + 1, 1])
            s["x"], s["u"] = _ln_step(s["x"], h, mods[i, 2], ln_g[i], ln_b[i], nxt, alpha,
                                      s["row0"], s["stride"], s["tt"])
    return streams[0]["x"], streams[1]["x"], jnp.stack(new_states, axis=1)
```

```python
import functools
import math

import numpy as np
import jax
import jax.numpy as jnp
from jax import lax
from jax.experimental import pallas as pl
from jax.experimental.pallas import tpu as pltpu

GRID_W = 64
N_MIXERS = 3
LN_EPS = 1e-5
HY_BANDS = 16
HY_TARGET = 1e-2
HY_FAST_PCT = 0.3
HY_SLOW_PCT = 1.5
HY_MIN_DECAY = math.log(HY_TARGET) / HY_SLOW_PCT
HY_MAX_DECAY = math.log(HY_TARGET) / HY_FAST_PCT
GLA_HEADS = 4
GLA_TAU = 16.0
GLA_NORM_EPS = 1e-6
LOG2_E = math.log2(math.e)

LANES = 128
SUBLANES = 8
MOD_ROWS = 16
VMEM_LIMIT = 56 * 1024 * 1024

TM = 1024
TN = 1024
CT_FUSED = 256
TT = 256
WT_FFT = 1024
FFT_S = 2 * SUBLANES
FFT_SF = SUBLANES
FFT_DIRECT_MAX = 512
GLA_CHUNK = 256
GLA_DIAG = 4

F32 = jnp.float32
BF16 = jnp.bfloat16
HIGHEST = lax.Precision.HIGHEST


def _params(*sem):
    return pltpu.CompilerParams(dimension_semantics=sem, vmem_limit_bytes=VMEM_LIMIT)


def _silu(x):
    return x / (1.0 + jnp.exp(-x))


def _mod_kernel(c_ref, w_ref, b_ref, o_ref):
    s = _silu(c_ref[...])
    o_ref[...] = jnp.dot(s, w_ref[...], precision=HIGHEST, preferred_element_type=F32) + b_ref[...]


def _modulation(cond, ada_w, ada_b):
    depth, d, n = ada_w.shape
    tn = 512
    return pl.pallas_call(
        _mod_kernel,
        grid=(depth, n // tn),
        in_specs=[
            pl.BlockSpec((MOD_ROWS, d), lambda i, j: (0, 0)),
            pl.BlockSpec((None, d, tn), lambda i, j: (i, 0, j)),
            pl.BlockSpec((None, 1, tn), lambda i, j: (i, 0, j)),
        ],
        out_specs=pl.BlockSpec((None, MOD_ROWS, tn), lambda i, j: (i, 0, j)),
        out_shape=jax.ShapeDtypeStruct((depth, MOD_ROWS, n), F32),
        compiler_params=_params("parallel", "parallel"),
    )(cond, ada_w, ada_b.reshape(depth, 1, n))


def _premod_kernel(x_ref, sh_ref, sc_ref, u_ref):
    u_ref[...] = (x_ref[...] * (1.0 + sc_ref[...]) + sh_ref[...]).astype(BF16)


def _row_spec(d, row0, stride):
    return pl.BlockSpec((None, 1, d), lambda b, t: (row0 + stride * b, 0, 0))


def _premod(x, shift, scale, row0, stride, tt):
    bsz, length, d = x.shape
    tok = pl.BlockSpec((None, tt, d), lambda b, t: (b, t, 0))
    return pl.pallas_call(
        _premod_kernel,
        grid=(bsz, length // tt),
        in_specs=[tok, _row_spec(d, row0, stride), _row_spec(d, row0, stride)],
        out_specs=tok,
        out_shape=jax.ShapeDtypeStruct(x.shape, BF16),
        compiler_params=_params("parallel", "parallel"),
    )(x, shift, scale)


def _mm_kernel(a_ref, b_ref, o_ref):
    o_ref[...] = jnp.dot(a_ref[...].astype(BF16), b_ref[...], preferred_element_type=F32).astype(o_ref.dtype)


def _matmul(a, b, out_dtype=F32):
    m, k = a.shape
    n = b.shape[1]
    tm = min(TM if a.dtype == BF16 else TM // 2, m)
    tn = min(TN, n)
    assert m % tm == 0 and n % tn == 0
    return pl.pallas_call(
        _mm_kernel,
        grid=(n // tn, m // tm),
        in_specs=[
            pl.BlockSpec((tm, k), lambda j, i: (i, 0)),
            pl.BlockSpec((k, tn), lambda j, i: (0, j)),
        ],
        out_specs=pl.BlockSpec((tm, tn), lambda j, i: (i, j)),
        out_shape=jax.ShapeDtypeStruct((m, n), out_dtype),
        compiler_params=_params("parallel", "parallel"),
    )(a, b)


def _conv3_rows(x, w_ref, period):
    rows = x.shape[0]
    pos = lax.broadcasted_iota(jnp.int32, x.shape, 0) % period
    prev = jnp.where(pos == 0, 0.0, pltpu.roll(x, 1, 0))
    nxt = jnp.where(pos == period - 1, 0.0, pltpu.roll(x, rows - 1, 0))
    return prev * w_ref[0:1, :] + x * w_ref[1:2, :] + nxt * w_ref[2:3, :]


def _inproj_sconv_kernel(a_ref, wb_ref, wc_ref, wx_ref, wg_ref, cw_ref, z_ref, *, period):
    a = a_ref[...]
    dot = lambda w_ref: jnp.dot(a, w_ref[...], preferred_element_type=F32)
    y = dot(wb_ref) * _conv3_rows(dot(wc_ref) * dot(wx_ref), cw_ref, period)
    z_ref[...] = (y * _silu(dot(wg_ref))).astype(BF16)


def _inproj_hyena_kernel(a_ref, w0_ref, w1_ref, w2_ref, wg_ref, c0_ref, c1_ref, c2_ref,
                         b0_ref, b1_ref, b2_ref, v_ref, xg_ref, *, period):
    a = a_ref[...]
    dot = lambda w_ref: jnp.dot(a, w_ref[...], preferred_element_type=F32)
    x1 = _conv3_rows(dot(w1_ref), c1_ref, period) + b1_ref[...]
    v = _conv3_rows(dot(w2_ref), c2_ref, period) + b2_ref[...]
    v_ref[...] = v * x1
    x0 = _conv3_rows(dot(w0_ref), c0_ref, period) + b0_ref[...]
    xg_ref[...] = x0 * _silu(dot(wg_ref))


def _inproj_fused(u, w_in, kind, conv_w, conv_b, period):
    t, d = u.shape
    w = w_in.shape[1] // 4
    ct = min(CT_FUSED, w)
    tm = min(TM, t)
    nb = w // ct
    assert t % tm == 0 and tm % period == 0

    def col(rows, k):
        return pl.BlockSpec((rows, ct), lambda j, i, k=k: (0, k * nb + j))

    a_spec = pl.BlockSpec((tm, d), lambda j, i: (i, 0))
    out = pl.BlockSpec((tm, ct), lambda j, i: (i, j))
    w_specs = [col(d, k) for k in range(4)]
    if kind == "sconv":
        return pl.pallas_call(
            functools.partial(_inproj_sconv_kernel, period=period),
            grid=(nb, t // tm),
            in_specs=[a_spec] + w_specs + [col(3, 0)],
            out_specs=out,
            out_shape=jax.ShapeDtypeStruct((t, w), BF16),
            compiler_params=_params("parallel", "parallel"),
        )(u, w_in, w_in, w_in, w_in, conv_w)
    cb = conv_b.reshape(1, 3 * w)
    return pl.pallas_call(
        functools.partial(_inproj_hyena_kernel, period=period),
        grid=(nb, t // tm),
        in_specs=[a_spec] + w_specs + [col(3, k) for k in range(3)] + [col(1, k) for k in range(3)],
        out_specs=[out, out],
        out_shape=[jax.ShapeDtypeStruct((t, w), F32)] * 2,
        compiler_params=_params("parallel", "parallel"),
    )(u, w_in, w_in, w_in, w_in, conv_w, conv_w, conv_w, cb, cb, cb)


def _ln_kernel(*refs, alpha, with_next):
    if with_next:
        x_ref, h_ref, gt_ref, g_ref, b_ref, sh_ref, sc_ref, xo_ref, uo_ref = refs
    else:
        x_ref, h_ref, gt_ref, g_ref, b_ref, xo_ref = refs
    y = alpha * x_ref[...] + gt_ref[...] * h_ref[...]
    mu = jnp.mean(y, axis=-1, keepdims=True)
    dlt = y - mu
    var = jnp.mean(dlt * dlt, axis=-1, keepdims=True)
    xn = dlt * lax.rsqrt(var + LN_EPS) * g_ref[...] + b_ref[...]
    xo_ref[...] = xn
    if with_next:
        uo_ref[...] = (xn * (1.0 + sc_ref[...]) + sh_ref[...]).astype(BF16)


def _ln_step(x, h, gate, ln_g, ln_b, nxt, alpha, row0, stride, tt):
    bsz, length, d = x.shape
    tok = pl.BlockSpec((None, tt, d), lambda b, t: (b, t, 0))
    row = _row_spec(d, row0, stride)
    vec = pl.BlockSpec((1, d), lambda b, t: (0, 0))
    ins = [x, h.reshape(x.shape), gate, ln_g.reshape(1, d), ln_b.reshape(1, d)]
    specs = [tok, tok, row, vec, vec]
    out_shape = [jax.ShapeDtypeStruct(x.shape, F32)]
    out_specs = [tok]
    if nxt is not None:
        ins += [nxt[0], nxt[1]]
        specs += [row, row]
        out_shape.append(jax.ShapeDtypeStruct(x.shape, BF16))
        out_specs.append(tok)
    res = pl.pallas_call(
        functools.partial(_ln_kernel, alpha=alpha, with_next=nxt is not None),
        grid=(bsz, length // tt),
        in_specs=specs,
        out_specs=out_specs,
        out_shape=out_shape,
        compiler_params=_params("parallel", "parallel"),
    )(*ins)
    return (res[0], res[1]) if nxt is not None else (res[0], None)


def _filter_kernel(band_ref, sel_ref, w_in_ref, b_in_ref, w_mid_ref, b_mid_ref, fr_ref,
                   w_out_ref, b_out_ref, dl_ref, k_ref, *, length, tr):
    n = pl.program_id(0) * tr + lax.broadcasted_iota(jnp.int32, (tr, 1), 0)
    pos = jnp.where(n < length, n, 2 * length - n)
    t = pos.astype(F32) / length
    ang = (2.0 * math.pi) * t * band_ref[...]
    sel = sel_ref[...]
    z = jnp.where(sel == 0, t, jnp.where(sel == 1, jnp.cos(ang), jnp.where(sel == 2, -jnp.sin(ang), 0.0)))
    fr = fr_ref[...]
    h = jnp.sin(fr * (jnp.dot(z, w_in_ref[...], precision=HIGHEST, preferred_element_type=F32) + b_in_ref[...]))
    for m in range(w_mid_ref.shape[0]):
        h = jnp.sin(fr * (jnp.dot(h, w_mid_ref[m], precision=HIGHEST, preferred_element_type=F32)
                          + b_mid_ref[m]))
    o = jnp.dot(h, w_out_ref[...], precision=HIGHEST, preferred_element_type=F32) + b_out_ref[...]
    o = o * jnp.exp(-t * dl_ref[...])
    k_ref[...] = jnp.where(n == length, 0.0, o)


def _hyena_filter(length, width, fw_in, fb_in, fw_mid, fb_mid, freq, fw_out, fb_out):
    hid = fw_in.shape[1]
    n_mid = fw_mid.shape[0]
    emb = fw_in.shape[0]
    hp = LANES
    bands = jnp.linspace(1e-4, HY_BANDS - 1, HY_BANDS, dtype=F32)
    band_row = jnp.zeros((1, hp), F32).at[0, 1:1 + HY_BANDS].set(bands).at[0, 1 + HY_BANDS:emb].set(bands)
    sel_np = np.full((1, hp), 3, np.int32)
    sel_np[0, 0] = 0
    sel_np[0, 1:1 + HY_BANDS] = 1
    sel_np[0, 1 + HY_BANDS:emb] = 2
    w_in = jnp.zeros((hp, hp), F32).at[:emb, :hid].set(fw_in)
    b_in = jnp.zeros((1, hp), F32).at[0, :hid].set(fb_in)
    w_mid = jnp.zeros((n_mid, hp, hp), F32).at[:, :hid, :hid].set(fw_mid)
    b_mid = jnp.zeros((n_mid, 1, hp), F32).at[:, 0, :hid].set(fb_mid)
    fr = jnp.zeros((1, hp), F32).at[0, :hid].set(freq)
    w_out = jnp.zeros((hp, 2 * width), F32).at[:hid].set(fw_out)
    deltas = jnp.abs(jnp.linspace(HY_MIN_DECAY, HY_MAX_DECAY, width, dtype=F32)).reshape(1, width)
    n_rows = 2 * length
    tr = min(256, length)
    half = length // tr
    const2 = lambda i: (0, 0)
    const3 = lambda i: (0, 0, 0)
    side = lambda i: (0, jnp.where(i < half, 0, 1))
    return pl.pallas_call(
        functools.partial(_filter_kernel, length=length, tr=tr),
        grid=(n_rows // tr,),
        in_specs=[
            pl.BlockSpec((1, hp), const2), pl.BlockSpec((1, hp), const2),
            pl.BlockSpec((hp, hp), const2), pl.BlockSpec((1, hp), const2),
            pl.BlockSpec((n_mid, hp, hp), const3), pl.BlockSpec((n_mid, 1, hp), const3),
            pl.BlockSpec((1, hp), const2),
            pl.BlockSpec((hp, width), side), pl.BlockSpec((1, width), side),
            pl.BlockSpec((1, width), const2),
        ],
        out_specs=pl.BlockSpec((tr, width), lambda i: (i, 0)),
        out_shape=jax.ShapeDtypeStruct((n_rows, width), F32),
        compiler_params=_params("parallel"),
    )(band_row, jnp.asarray(sel_np), w_in, b_in, w_mid, b_mid, fr, w_out,
      fb_out.reshape(1, 2 * width), deltas)


def _hi_lo(a):
    a = np.asarray(a, np.float32)
    hi = a.astype(BF16)
    lo = (a - hi.astype(np.float32)).astype(BF16)
    return hi, lo


def _rot(c, s, sign):
    return [[c, sign * s], [-sign * s, c]]


@functools.lru_cache(maxsize=None)
def _direct_tables(length):
    n = 2 * length
    th = 2 * np.pi * np.outer(np.arange(n), np.arange(n)) / n
    c, s = np.cos(th), np.sin(th)
    fwd = np.block(_rot(c[:, :length], s[:, :length], 1.0))
    flt = np.concatenate([c, -s], 0)
    inv = np.block(_rot(c[:length] / n, s[:length] / n, -1.0))
    return _hi_lo(fwd), _hi_lo(flt), _hi_lo(inv)


def _fft_split(n):
    n1 = 64 if n >= 8192 else 16
    return n1, n // n1


@functools.lru_cache(maxsize=None)
def _stage_tables(n1, n2):
    n = n1 * n2
    h1 = n1 // 2
    s_ = FFT_S
    k1 = np.arange(n1, dtype=np.float64)
    eye = np.eye(s_)

    def stage1(cols, sg):
        n2v = (np.arange(n2 // sg)[:, None] * sg + np.arange(sg)[None, :])[:, None, :, None]
        th = 2 * np.pi * (k1[None, :, None, None] * np.arange(cols)[None, None, None, :] / n1
                          + n2v * k1[None, :, None, None] / n)
        return np.cos(th), np.sin(th)

    def spread(blocks, rows, cols, sg):
        nc, ncp = len(blocks), len(blocks[0])
        out = np.zeros((n2 // sg, nc, rows, sg, ncp, cols, sg))
        for ci in range(nc):
            for cj in range(ncp):
                out[:, ci, :, :, cj, :, :] = np.einsum("jksm,st->jksmt", blocks[ci][cj], np.eye(sg))
        return out.reshape(n2 // sg, nc * rows * sg, ncp * cols * sg)

    c, s = stage1(h1, s_)
    m1 = spread(_rot(c, s, 1.0), n1, h1, s_)
    c, s = stage1(n1, FFT_SF)
    m1f = spread([[c], [-s]], n1, n1, FFT_SF)
    ph = 2 * np.pi * np.outer(np.arange(n2), np.arange(n2)) / n2
    t2 = np.block(_rot(np.cos(ph), np.sin(ph), 1.0))
    ps = 2 * np.pi * (np.outer(np.arange(n2), np.arange(n2))[None] / n2
                      + k1[:, None, None] * np.arange(n2)[None, :, None] / n)
    c, s = np.cos(ps), np.sin(ps)
    t2i = np.concatenate([np.concatenate([c, -s], 2), np.concatenate([s, c], 2)], 1)
    th = 2 * np.pi * np.outer(np.arange(h1), k1) / n1
    c, s = np.cos(th) / n, np.sin(th) / n
    blk = _rot(c, s, -1.0)
    m3 = np.zeros((2, h1, s_, 2, n1, s_))
    for ci in range(2):
        for cj in range(2):
            m3[ci, :, :, cj, :, :] = np.einsum("nk,st->nskt", blk[ci][cj], eye)
    m3 = m3.reshape(2 * h1 * s_, 2 * n1 * s_)
    return _hi_lo(m1), _hi_lo(m1f), _hi_lo(t2), _hi_lo(t2i), _hi_lo(m3)


def _split_bf16(x):
    hi = x.astype(BF16)
    return hi, (x - hi.astype(F32)).astype(BF16)


def _dot3(mh, ml, x):
    xh, xl = _split_bf16(x)
    dot = lambda a, b: jnp.dot(a, b, preferred_element_type=F32)
    return dot(mh, xh) + dot(mh, xl) + dot(ml, xh)


def _dot1(mh, x):
    return jnp.dot(mh, x.astype(BF16), preferred_element_type=F32)


def _tabmm_kernel(*refs, passes):
    x_ref, o_ref = refs[0], refs[-1]
    x = x_ref[...].reshape(-1, x_ref.shape[-1])
    y = _dot1(refs[1][...], x) if passes == 1 else _dot3(refs[1][...], refs[2][...], x)
    o_ref[...] = y.reshape(o_ref.shape).astype(o_ref.dtype)


def _tabmm(x, x_spec, tab, tab_spec, out_shape, out_spec, grid, passes, out_dtype=F32):
    tabs = [jnp.asarray(tab[0])] + ([jnp.asarray(tab[1])] if passes == 3 else [])
    return pl.pallas_call(
        functools.partial(_tabmm_kernel, passes=passes),
        grid=grid,
        in_specs=[x_spec] + [tab_spec] * len(tabs),
        out_specs=out_spec,
        out_shape=jax.ShapeDtypeStruct(out_shape, out_dtype),
        compiler_params=_params(*(["parallel"] * len(grid))),
    )(x, *tabs)


def _cmul_rows(s, kf_ref):
    n = s.shape[0] // 2
    sr, si = s[:n], s[n:]
    kr, ki = kf_ref[0], kf_ref[1]
    return jnp.concatenate([sr * kr - si * ki, sr * ki + si * kr], axis=0)


def _fft_mid_kernel(x_ref, kf_ref, t_ref, i_ref, o_ref):
    x = x_ref[...].reshape(-1, x_ref.shape[-1])
    y = _cmul_rows(_dot1(t_ref[...], x), kf_ref)
    o_ref[...] = _dot1(i_ref[...], y).reshape(o_ref.shape).astype(o_ref.dtype)


def _fft_s3_kernel(b_ref, m_ref, v_ref, g_ref, sk_ref, z_ref):
    x = b_ref[...].reshape(-1, b_ref.shape[-1])
    y = _dot1(m_ref[...], x).reshape(z_ref.shape)
    z_ref[...] = g_ref[...] * (y + v_ref[...] * sk_ref[...])


def _direct_conv_kernel(v_ref, g_ref, kf_ref, f_ref, i_ref, sk_ref, z_ref):
    x = v_ref[...].reshape(-1, v_ref.shape[-1])
    y = _cmul_rows(_dot1(f_ref[...], x), kf_ref)
    o = _dot1(i_ref[...], y).reshape(z_ref.shape)
    z_ref[...] = g_ref[...] * (o + v_ref[...] * sk_ref[...])


def _long_conv_direct(vprime, xg, kfull, skip, bsz, length, width):
    n = 2 * length
    wt = min(WT_FFT // 2, width)
    nw = width // wt
    fwd, flt, inv = _direct_tables(length)
    kf = _tabmm(kfull, pl.BlockSpec((n, wt), lambda j: (0, j)), flt,
                pl.BlockSpec((2 * n, n), lambda j: (0, 0)),
                (2, n, width), pl.BlockSpec((2, n, wt), lambda j: (0, 0, j)), (nw,), 3)
    tok = pl.BlockSpec((2, length, wt), lambda j, p: (p, 0, j))
    const = lambda shape: pl.BlockSpec(shape, lambda j, p: (0, 0))
    z = pl.pallas_call(
        _direct_conv_kernel,
        grid=(nw, bsz // 2),
        in_specs=[tok, tok, pl.BlockSpec((2, n, wt), lambda j, p: (0, 0, j)),
                  const((2 * n, n)), const((n, 2 * n)),
                  pl.BlockSpec((1, wt), lambda j, p: (0, j))],
        out_specs=tok,
        out_shape=jax.ShapeDtypeStruct((bsz, length, width), F32),
        compiler_params=_params("parallel", "parallel"),
    )(vprime.reshape(bsz, length, width), xg.reshape(bsz, length, width), kf,
      jnp.asarray(fwd[0]), jnp.asarray(inv[0]), skip.reshape(1, width))
    return z.reshape(bsz * length, width)


def _long_conv_staged(vprime, xg, kfull, skip, bsz, length, width):
    n = 2 * length
    n1, n2 = _fft_split(n)
    h1 = n1 // 2
    s_ = FFT_S
    nj = n2 // s_
    wt = min(WT_FFT, width)
    nw = width // wt
    pairs = bsz // 2
    m1, m1f, t2, t2i, m3 = _stage_tables(n1, n2)
    spec5 = lambda sg: pl.BlockSpec((None, 2, n1, sg, wt), lambda j, w, p: (p, 0, 0, j, w))
    tab_j = lambda tab: pl.BlockSpec((None,) + tab[0].shape[1:], lambda j, w, p: (j, 0, 0))
    sf = FFT_SF
    ka = _tabmm(kfull.reshape(n1, n2, width), pl.BlockSpec((n1, sf, wt), lambda j, w, p: (0, j, w)),
                m1f, tab_j(m1f), (1, 2, n1, n2, width), spec5(sf), (n2 // sf, nw, 1), 3)
    kf = _tabmm(ka, pl.BlockSpec((None, 2, None, n2, wt), lambda k, w: (0, 0, k, 0, w)), t2,
                pl.BlockSpec((2 * n2, 2 * n2), lambda k, w: (0, 0)),
                (2, n1, n2, width), pl.BlockSpec((2, None, n2, wt), lambda k, w: (0, k, 0, w)), (n1, nw), 3)
    v4 = vprime.reshape(bsz, h1, n2, width)
    g4 = xg.reshape(bsz, h1, n2, width)
    tok = pl.BlockSpec((2, h1, s_, wt), lambda j, w, p: (p, 0, j, w))
    a = _tabmm(v4, tok, m1, tab_j(m1), (pairs, 2, n1, n2, width), spec5(s_), (nj, nw, pairs), 1, BF16)
    wm = min(2 * WT_FFT, width)
    blk = pl.BlockSpec((None, 2, None, n2, wm), lambda k, w, p: (p, 0, k, 0, w))
    sq = 2 * n2
    bm = pl.pallas_call(
        _fft_mid_kernel,
        grid=(n1, width // wm, pairs),
        in_specs=[blk, pl.BlockSpec((2, None, n2, wm), lambda k, w, p: (0, k, 0, w)),
                  pl.BlockSpec((sq, sq), lambda k, w, p: (0, 0)),
                  pl.BlockSpec((None, sq, sq), lambda k, w, p: (k, 0, 0))],
        out_specs=blk,
        out_shape=jax.ShapeDtypeStruct((pairs, 2, n1, n2, width), BF16),
        compiler_params=_params("parallel", "parallel", "parallel"),
    )(a, kf, jnp.asarray(t2[0]), jnp.asarray(t2i[0]))
    m3_spec = pl.BlockSpec(m3[0].shape, lambda j, w, p: (0, 0))
    z = pl.pallas_call(
        _fft_s3_kernel,
        grid=(nj, nw, pairs),
        in_specs=[spec5(s_), m3_spec, tok, tok,
                  pl.BlockSpec((1, wt), lambda j, w, p: (0, w))],
        out_specs=tok,
        out_shape=jax.ShapeDtypeStruct((bsz, h1, n2, width), F32),
        compiler_params=_params("parallel", "parallel", "parallel"),
    )(bm, jnp.asarray(m3[0]), v4, g4, skip.reshape(1, width))
    return z.reshape(bsz * length, width)


def _hyena_core(vprime, xg, bsz, length, hy):
    fw_in, fb_in, fw_mid, fb_mid, freq, fw_out, fb_out, skip = hy
    width = vprime.shape[1]
    kfull = _hyena_filter(length, width, fw_in, fb_in, fw_mid, fb_mid, freq, fw_out, fb_out)
    conv = _long_conv_direct if length <= FFT_DIRECT_MAX else _long_conv_staged
    return conv(vprime, xg, kfull, skip, bsz, length, width)


def _gla_levels(cc):
    lv = []
    m = cc // 2
    while m >= GLA_DIAG:
        lv.append(m)
        m //= 2
    return lv


def _gla_level_map(cc, reverse):
    t = np.arange(cc)[:, None]
    s = np.arange(cc)[None, :]
    out = np.full((cc, cc), -1, np.int32)
    for idx, m in enumerate(_gla_levels(cc)):
        same = (t // (2 * m)) == (s // (2 * m))
        t_hi = (t % (2 * m)) >= m
        s_hi = (s % (2 * m)) >= m
        own = same & (t_hi & ~s_hi if not reverse else ~t_hi & s_hi)
        out[own] = idx
    return out


def _gla_scan_kernel(*refs, reverse, has_init, keep_state, finish, scale, cc):
    refs = list(refs)
    q_ref, k_ref, v_ref, low_ref, wg_ref, bg_ref, lv_ref, tri_ref = refs[:8]
    pos = 8
    if has_init:
        s0_ref = refs[pos]
        pos += 1
    if finish:
        of_ref, r_ref, ng_ref = refs[pos:pos + 3]
        pos += 3
    if keep_state and reverse:
        pos += 1
    o_ref = refs[pos]
    sf_ref = refs[pos + 1] if keep_state else None
    st_ref, b_ref = refs[-2:]
    c = pl.program_id(2)
    dk = q_ref.shape[1]

    @pl.when(c == 0)
    def _():
        if has_init:
            st_ref[...] = s0_ref[...].T
        else:
            st_ref[...] = jnp.zeros_like(st_ref)

    logit = jnp.dot(low_ref[...].astype(BF16), wg_ref[...], preferred_element_type=F32) + bg_ref[...]
    g = (jnp.minimum(logit, 0.0) - jnp.log(1.0 + jnp.exp(-jnp.abs(logit)))) * (LOG2_E / GLA_TAU)

    row = lax.broadcasted_iota(jnp.int32, (cc, dk), 0)
    g_hi = g.astype(BF16)
    rest = g - g_hi.astype(F32)
    g_mid = rest.astype(BF16)
    g_lo = (rest - g_mid.astype(F32)).astype(BF16)
    b = jnp.dot(tri_ref[...], jnp.concatenate([g_hi, g_mid, g_lo], axis=0), preferred_element_type=F32)
    b_ref[...] = b
    q = q_ref[...] * scale
    kk = k_ref[...]
    vb = v_ref[...].astype(BF16)
    state = st_ref[...]

    edge = 0 if reverse else cc - 1
    b_edge = b_ref[edge:edge + 1, :]
    o = lax.dot_general((q * jnp.exp2(b)).astype(BF16), state.astype(BF16),
                        (((1,), (1,)), ((), ())), preferred_element_type=F32)
    kd = (kk * jnp.exp2(b_edge - b)).astype(BF16)
    upd = lax.dot_general(vb, kd, (((0,), (0,)), ((), ())), preferred_element_type=F32)
    st_ref[...] = state * jnp.exp2(b_edge) + upd

    lvm = lv_ref[...]
    attn = jnp.zeros((cc, cc), F32)
    for idx, m in enumerate(_gla_levels(cc)):
        pieces = []
        for blk in range(cc // (2 * m)):
            ref_row = blk * 2 * m + (m if reverse else m - 1)
            pieces.append(jnp.broadcast_to(b_ref[ref_row:ref_row + 1, :], (2 * m, dk)))
        rr = pieces[0] if len(pieces) == 1 else jnp.concatenate(pieces, axis=0)
        later = (row & m) != 0
        qside = jnp.logical_not(later) if reverse else later
        w = jnp.exp2(jnp.minimum(jnp.where(qside, b - rr, rr - b), 0.0))
        z = (jnp.where(qside, q, kk) * w).astype(BF16)
        p = lax.dot_general(z, z, (((1,), (1,)), ((), ())), preferred_element_type=F32)
        attn = jnp.where(lvm == idx, p, attn)

    ti = lax.broadcasted_iota(jnp.int32, (cc, cc), 0)
    si = lax.broadcasted_iota(jnp.int32, (cc, cc), 1)
    tin = ti & (GLA_DIAG - 1)
    diag = jnp.zeros((cc, cc), F32)
    for dlt in range(GLA_DIAG):
        if dlt == 0:
            term = q * kk
        else:
            shift = cc - dlt if reverse else dlt
            decay = jnp.exp2(jnp.minimum(b - pltpu.roll(b, shift, 0), 0.0))
            term = q * pltpu.roll(kk, shift, 0) * decay
        col = jnp.sum(term, axis=1, keepdims=True)
        if reverse:
            own = (si == ti + dlt) & (tin < GLA_DIAG - dlt)
        else:
            own = (si == ti - dlt) & (tin >= dlt)
        diag = jnp.where(own, col, diag)
    attn = jnp.where(lvm < 0, diag, attn)
    o = o + jnp.dot(attn.astype(BF16), vb, preferred_element_type=F32)

    if finish:
        o = o + of_ref[...]
        o = o * lax.rsqrt(jnp.mean(o * o, axis=-1, keepdims=True) + GLA_NORM_EPS)
        o_ref[...] = (o * ng_ref[...] * _silu(r_ref[...])).astype(o_ref.dtype)
    else:
        o_ref[...] = o

    if keep_state:
        @pl.when(c == pl.num_programs(2) - 1)
        def _():
            sf_ref[...] = st_ref[...].T


def _gla_scan(proj, low, wdir, bg, norm_g, init, o_first, states, bsz, length, dk, dv, direction, cc):
    reverse = direction == 1
    finish = o_first is not None
    keep_state = init is None
    nh = GLA_HEADS
    nc = length // cc
    t = bsz * length
    kblk = nh
    vblk = (2 * nh * dk) // dv
    rblk = (2 * nh * dk + nh * dv) // dv
    lw = low.shape[1]

    def chunk(b, c):
        return b * nc + (nc - 1 - c if reverse else c)

    head = pl.BlockSpec((cc, dv), lambda b, h, c: (chunk(b, c), h))
    in_specs = [
        pl.BlockSpec((cc, dk), lambda b, h, c: (chunk(b, c), h)),
        pl.BlockSpec((cc, dk), lambda b, h, c: (chunk(b, c), kblk + h)),
        pl.BlockSpec((cc, dv), lambda b, h, c: (chunk(b, c), vblk + h)),
        pl.BlockSpec((cc, lw), lambda b, h, c: (chunk(b, c), 0)),
        pl.BlockSpec((None, lw, dk), lambda b, h, c: (direction, 0, h)),
        pl.BlockSpec((None, 1, dk), lambda b, h, c: (direction, 0, h)),
        pl.BlockSpec((cc, cc), lambda b, h, c: (0, 0)),
        pl.BlockSpec((cc, 3 * cc), lambda b, h, c: (0, 0)),
    ]
    tri = np.triu(np.ones((cc, cc), np.float32)) if reverse else np.tril(np.ones((cc, cc), np.float32))
    ins = [proj, proj, proj, low, wdir, bg, jnp.asarray(_gla_level_map(cc, reverse)),
           jnp.asarray(np.tile(tri, (1, 3)), dtype=BF16)]
    if init is not None:
        state, layer = init
        in_specs.append(pl.BlockSpec((None, None, None, None, dk, dv),
                                     lambda b, h, c: (b, layer, direction, h, 0, 0)))
        ins.append(state)
    if finish:
        in_specs += [head, pl.BlockSpec((cc, dv), lambda b, h, c: (chunk(b, c), rblk + h)),
                     pl.BlockSpec((1, dv), lambda b, h, c: (0, 0))]
        ins += [o_first, proj, norm_g.reshape(1, dv)]
    out_specs = [head]
    out_shape = [jax.ShapeDtypeStruct((t, nh * dv), BF16 if finish else F32)]
    aliases = {}
    if keep_state:
        out_specs.append(pl.BlockSpec((None, None, None, dk, dv), lambda b, h, c: (b, direction, h, 0, 0)))
        out_shape.append(jax.ShapeDtypeStruct((bsz, 2, nh, dk, dv), F32))
        if states is not None:
            in_specs.append(pl.BlockSpec(memory_space=pl.ANY))
            ins.append(states)
            aliases = {len(ins) - 1: 1}
    res = pl.pallas_call(
        functools.partial(_gla_scan_kernel, reverse=reverse, has_init=init is not None,
                          keep_state=keep_state, finish=finish, scale=dk ** -0.5, cc=cc),
        grid=(bsz, nh, nc),
        in_specs=in_specs,
        out_specs=out_specs,
        out_shape=out_shape,
        input_output_aliases=aliases,
        scratch_shapes=[pltpu.VMEM((dv, dk), F32), pltpu.VMEM((cc, dk), F32)],
        compiler_params=_params("parallel", "parallel", "arbitrary"),
    )(*ins)
    return (res[0], res[1]) if keep_state else (res[0], None)


def _gla_core(u, proj, bsz, length, init, gla):
    wg1, wg2, bg, norm_g = gla
    d = u.shape[1]
    rank = wg1.shape[2]
    dk_tot = wg2.shape[2]
    dk = dk_tot // GLA_HEADS
    dv = norm_g.shape[0]
    wlow = jnp.zeros((d, LANES), F32).at[:, :rank].set(wg1[0]).at[:, rank:2 * rank].set(wg1[1])
    low = _matmul(u, wlow.astype(BF16))
    wdir = jnp.zeros((2, LANES, dk_tot), F32).at[0, :rank].set(wg2[0]).at[1, rank:2 * rank].set(wg2[1])
    wdir = wdir.astype(BF16)
    cc = min(GLA_CHUNK, length)
    o, states = None, None
    for direction in (0, 1):
        o, states = _gla_scan(proj, low, wdir, bg.reshape(2, 1, dk_tot), norm_g, init, o, states,
                              bsz, length, dk, dv, direction, cc)
    return o, states


def kernel(x_prompt, x_sample, state_gla, c, c_ctx, ada_w, ada_b, ln_g, ln_b,
           hy_w_in, hy_conv_w, hy_conv_b, hy_fw_in, hy_fb_in, hy_fw_mid, hy_fb_mid,
           hy_freq, hy_fw_out, hy_fb_out, hy_skip, hy_w_out,
           sc_w_in, sc_conv_w, sc_w_out,
           gla_w_in, gla_wg1, gla_wg2, gla_bg, gla_norm_g, gla_w_out):
    depth, d, _ = ada_w.shape
    alpha = (2 * depth) ** 0.25
    bp, lp, _ = x_prompt.shape
    bs, ls, _ = x_sample.shape
    assert 1 + bs <= MOD_ROWS

    cond = jnp.zeros((MOD_ROWS, d), F32).at[0].set(c_ctx).at[1:1 + bs].set(c)
    mods = _modulation(cond, ada_w, ada_b)
    mods = mods.reshape(depth, MOD_ROWS, 3, 1, d).transpose(0, 2, 1, 3, 4)

    streams = [
        dict(x=x_prompt, b=bp, l=lp, period=lp, row0=0, stride=0, tt=min(TT, lp)),
        dict(x=x_sample, b=bs, l=ls, period=min(GRID_W, ls), row0=1, stride=1, tt=min(TT, ls)),
    ]
    for s in streams:
        s["u"] = _premod(s["x"], mods[0, 0], mods[0, 1], s["row0"], s["stride"], s["tt"])

    new_states = []
    for i in range(depth):
        kind = i % N_MIXERS
        j = i // N_MIXERS
        if kind == 0:
            w_in, w_out = hy_w_in[j].astype(BF16), hy_w_out[j].astype(BF16)
            hy = (hy_fw_in[j], hy_fb_in[j], hy_fw_mid[j], hy_fb_mid[j],
                  hy_freq[j], hy_fw_out[j], hy_fb_out[j], hy_skip[j])
        elif kind == 1:
            w_in, w_out = sc_w_in[j].astype(BF16), sc_w_out[j].astype(BF16)
        else:
            w_in, w_out = gla_w_in[j].astype(BF16), gla_w_out[j].astype(BF16)
            gla = (gla_wg1[j], gla_wg2[j], gla_bg[j], gla_norm_g[j])
        for si, s in enumerate(streams):
            u2 = s["u"].reshape(s["b"] * s["l"], d)
            if kind == 0:
                vprime, xg = _inproj_fused(u2, w_in, "hyena", hy_conv_w[j], hy_conv_b[j], s["period"])
                z = _hyena_core(vprime, xg, s["b"], s["l"], hy)
            elif kind == 1:
                z = _inproj_fused(u2, w_in, "sconv", sc_conv_w[j], None, s["period"])
            else:
                init = None if si == 0 else (state_gla, j)
                z, finals = _gla_core(u2, _matmul(u2, w_in), s["b"], s["l"], init, gla)
                if si == 0:
                    new_states.append(finals.astype(x_prompt.dtype))
            h = _matmul(z, w_out, BF16)
            nxt = None if i + 1 == depth else (mods[i + 1, 0], mods[i + 1, 1])
            s["x"], s["u"] = _ln_step(s["x"], h, mods[i, 2], ln_g[i], ln_b[i], nxt, alpha,
                                      s["row0"], s["stride"], s["tt"])
    return streams[0]["x"], streams[1]["x"], jnp.stack(new_states, axis=1)
```

```python
import functools
import math

import numpy as np
import jax
import jax.numpy as jnp
from jax import lax
from jax.experimental import pallas as pl
from jax.experimental.pallas import tpu as pltpu

GRID_W = 64
N_MIXERS = 3
LN_EPS = 1e-5
HY_BANDS = 16
HY_TARGET = 1e-2
HY_FAST_PCT = 0.3
HY_SLOW_PCT = 1.5
HY_MIN_DECAY = math.log(HY_TARGET) / HY_SLOW_PCT
HY_MAX_DECAY = math.log(HY_TARGET) / HY_FAST_PCT
GLA_HEADS = 4
GLA_TAU = 16.0
GLA_NORM_EPS = 1e-6
LOG2_E = math.log2(math.e)

LANES = 128
SUBLANES = 8
MOD_ROWS = 16
VMEM_LIMIT = 56 * 1024 * 1024

TM = 1024
TN = 1024
CT_FUSED = 256
TT = 256
WT_FFT = 1024
FFT_S = 2 * SUBLANES
FFT_SF = SUBLANES
FFT_DIRECT_MAX = 512
GLA_CHUNK = 256
GLA_HEADS_PER_STEP = 2
GLA_DIAG = 4

F32 = jnp.float32
BF16 = jnp.bfloat16
HIGHEST = lax.Precision.HIGHEST


def _params(*sem):
    return pltpu.CompilerParams(dimension_semantics=sem, vmem_limit_bytes=VMEM_LIMIT)


def _silu(x):
    return x / (1.0 + jnp.exp(-x))


def _mod_kernel(c_ref, w_ref, b_ref, o_ref):
    s = _silu(c_ref[...])
    o_ref[...] = jnp.dot(s, w_ref[...], precision=HIGHEST, preferred_element_type=F32) + b_ref[...]


def _modulation(cond, ada_w, ada_b):
    depth, d, n = ada_w.shape
    tn = 512
    return pl.pallas_call(
        _mod_kernel,
        grid=(depth, n // tn),
        in_specs=[
            pl.BlockSpec((MOD_ROWS, d), lambda i, j: (0, 0)),
            pl.BlockSpec((None, d, tn), lambda i, j: (i, 0, j)),
            pl.BlockSpec((None, 1, tn), lambda i, j: (i, 0, j)),
        ],
        out_specs=pl.BlockSpec((None, MOD_ROWS, tn), lambda i, j: (i, 0, j)),
        out_shape=jax.ShapeDtypeStruct((depth, MOD_ROWS, n), F32),
        compiler_params=_params("parallel", "parallel"),
    )(cond, ada_w, ada_b.reshape(depth, 1, n))


def _premod_kernel(x_ref, sh_ref, sc_ref, u_ref):
    u_ref[...] = (x_ref[...] * (1.0 + sc_ref[...]) + sh_ref[...]).astype(BF16)


def _row_spec(d, row0, stride):
    return pl.BlockSpec((None, 1, d), lambda b, t: (row0 + stride * b, 0, 0))


def _premod(x, shift, scale, row0, stride, tt):
    bsz, length, d = x.shape
    tok = pl.BlockSpec((None, tt, d), lambda b, t: (b, t, 0))
    return pl.pallas_call(
        _premod_kernel,
        grid=(bsz, length // tt),
        in_specs=[tok, _row_spec(d, row0, stride), _row_spec(d, row0, stride)],
        out_specs=tok,
        out_shape=jax.ShapeDtypeStruct(x.shape, BF16),
        compiler_params=_params("parallel", "parallel"),
    )(x, shift, scale)


def _mm_kernel(a_ref, b_ref, o_ref):
    o_ref[...] = jnp.dot(a_ref[...].astype(BF16), b_ref[...], preferred_element_type=F32).astype(o_ref.dtype)


def _matmul(a, b, out_dtype=F32):
    m, k = a.shape
    n = b.shape[1]
    tm = min(TM if a.dtype == BF16 else TM // 2, m)
    tn = min(TN, n)
    assert m % tm == 0 and n % tn == 0
    return pl.pallas_call(
        _mm_kernel,
        grid=(n // tn, m // tm),
        in_specs=[
            pl.BlockSpec((tm, k), lambda j, i: (i, 0)),
            pl.BlockSpec((k, tn), lambda j, i: (0, j)),
        ],
        out_specs=pl.BlockSpec((tm, tn), lambda j, i: (i, j)),
        out_shape=jax.ShapeDtypeStruct((m, n), out_dtype),
        compiler_params=_params("parallel", "parallel"),
    )(a, b)


def _conv3_rows(x, w_ref, period):
    rows = x.shape[0]
    pos = lax.broadcasted_iota(jnp.int32, x.shape, 0) % period
    prev = jnp.where(pos == 0, 0.0, pltpu.roll(x, 1, 0))
    nxt = jnp.where(pos == period - 1, 0.0, pltpu.roll(x, rows - 1, 0))
    return prev * w_ref[0:1, :] + x * w_ref[1:2, :] + nxt * w_ref[2:3, :]


def _inproj_sconv_kernel(a_ref, wb_ref, wc_ref, wx_ref, wg_ref, cw_ref, z_ref, *, period):
    a = a_ref[...]
    dot = lambda w_ref: jnp.dot(a, w_ref[...], preferred_element_type=F32)
    y = dot(wb_ref) * _conv3_rows(dot(wc_ref) * dot(wx_ref), cw_ref, period)
    z_ref[...] = (y * _silu(dot(wg_ref))).astype(BF16)


def _inproj_hyena_kernel(a_ref, w0_ref, w1_ref, w2_ref, wg_ref, c0_ref, c1_ref, c2_ref,
                         b0_ref, b1_ref, b2_ref, v_ref, xg_ref, *, period):
    a = a_ref[...]
    dot = lambda w_ref: jnp.dot(a, w_ref[...], preferred_element_type=F32)
    x1 = _conv3_rows(dot(w1_ref), c1_ref, period) + b1_ref[...]
    v = _conv3_rows(dot(w2_ref), c2_ref, period) + b2_ref[...]
    v_ref[...] = v * x1
    x0 = _conv3_rows(dot(w0_ref), c0_ref, period) + b0_ref[...]
    xg_ref[...] = x0 * _silu(dot(wg_ref))


def _inproj_fused(u, w_in, kind, conv_w, conv_b, period):
    t, d = u.shape
    w = w_in.shape[1] // 4
    ct = min(CT_FUSED, w)
    tm = min(TM, t)
    nb = w // ct
    assert t % tm == 0 and tm % period == 0

    def col(rows, k):
        return pl.BlockSpec((rows, ct), lambda j, i, k=k: (0, k * nb + j))

    a_spec = pl.BlockSpec((tm, d), lambda j, i: (i, 0))
    out = pl.BlockSpec((tm, ct), lambda j, i: (i, j))
    w_specs = [col(d, k) for k in range(4)]
    if kind == "sconv":
        return pl.pallas_call(
            functools.partial(_inproj_sconv_kernel, period=period),
            grid=(nb, t // tm),
            in_specs=[a_spec] + w_specs + [col(3, 0)],
            out_specs=out,
            out_shape=jax.ShapeDtypeStruct((t, w), BF16),
            compiler_params=_params("parallel", "parallel"),
        )(u, w_in, w_in, w_in, w_in, conv_w)
    cb = conv_b.reshape(1, 3 * w)
    return pl.pallas_call(
        functools.partial(_inproj_hyena_kernel, period=period),
        grid=(nb, t // tm),
        in_specs=[a_spec] + w_specs + [col(3, k) for k in range(3)] + [col(1, k) for k in range(3)],
        out_specs=[out, out],
        out_shape=[jax.ShapeDtypeStruct((t, w), F32)] * 2,
        compiler_params=_params("parallel", "parallel"),
    )(u, w_in, w_in, w_in, w_in, conv_w, conv_w, conv_w, cb, cb, cb)


def _ln_kernel(*refs, alpha, with_next):
    if with_next:
        x_ref, h_ref, gt_ref, g_ref, b_ref, sh_ref, sc_ref, xo_ref, uo_ref = refs
    else:
        x_ref, h_ref, gt_ref, g_ref, b_ref, xo_ref = refs
    y = alpha * x_ref[...] + gt_ref[...] * h_ref[...]
    mu = jnp.mean(y, axis=-1, keepdims=True)
    dlt = y - mu
    var = jnp.mean(dlt * dlt, axis=-1, keepdims=True)
    xn = dlt * lax.rsqrt(var + LN_EPS) * g_ref[...] + b_ref[...]
    xo_ref[...] = xn
    if with_next:
        uo_ref[...] = (xn * (1.0 + sc_ref[...]) + sh_ref[...]).astype(BF16)


def _ln_step(x, h, gate, ln_g, ln_b, nxt, alpha, row0, stride, tt):
    bsz, length, d = x.shape
    tok = pl.BlockSpec((None, tt, d), lambda b, t: (b, t, 0))
    row = _row_spec(d, row0, stride)
    vec = pl.BlockSpec((1, d), lambda b, t: (0, 0))
    ins = [x, h.reshape(x.shape), gate, ln_g.reshape(1, d), ln_b.reshape(1, d)]
    specs = [tok, tok, row, vec, vec]
    out_shape = [jax.ShapeDtypeStruct(x.shape, F32)]
    out_specs = [tok]
    if nxt is not None:
        ins += [nxt[0], nxt[1]]
        specs += [row, row]
        out_shape.append(jax.ShapeDtypeStruct(x.shape, BF16))
        out_specs.append(tok)
    res = pl.pallas_call(
        functools.partial(_ln_kernel, alpha=alpha, with_next=nxt is not None),
        grid=(bsz, length // tt),
        in_specs=specs,
        out_specs=out_specs,
        out_shape=out_shape,
        compiler_params=_params("parallel", "parallel"),
    )(*ins)
    return (res[0], res[1]) if nxt is not None else (res[0], None)


def _filter_kernel(band_ref, sel_ref, w_in_ref, b_in_ref, w_mid_ref, b_mid_ref, fr_ref,
                   w_out_ref, b_out_ref, dl_ref, k_ref, *, length, tr):
    n = pl.program_id(0) * tr + lax.broadcasted_iota(jnp.int32, (tr, 1), 0)
    pos = jnp.where(n < length, n, 2 * length - n)
    t = pos.astype(F32) / length
    ang = (2.0 * math.pi) * t * band_ref[...]
    sel = sel_ref[...]
    z = jnp.where(sel == 0, t, jnp.where(sel == 1, jnp.cos(ang), jnp.where(sel == 2, -jnp.sin(ang), 0.0)))
    fr = fr_ref[...]
    h = jnp.sin(fr * (jnp.dot(z, w_in_ref[...], precision=HIGHEST, preferred_element_type=F32) + b_in_ref[...]))
    for m in range(w_mid_ref.shape[0]):
        h = jnp.sin(fr * (jnp.dot(h, w_mid_ref[m], precision=HIGHEST, preferred_element_type=F32)
                          + b_mid_ref[m]))
    o = jnp.dot(h, w_out_ref[...], precision=HIGHEST, preferred_element_type=F32) + b_out_ref[...]
    o = o * jnp.exp(-t * dl_ref[...])
    k_ref[...] = jnp.where(n == length, 0.0, o)


def _hyena_filter(length, width, fw_in, fb_in, fw_mid, fb_mid, freq, fw_out, fb_out):
    hid = fw_in.shape[1]
    n_mid = fw_mid.shape[0]
    emb = fw_in.shape[0]
    hp = LANES
    bands = jnp.linspace(1e-4, HY_BANDS - 1, HY_BANDS, dtype=F32)
    band_row = jnp.zeros((1, hp), F32).at[0, 1:1 + HY_BANDS].set(bands).at[0, 1 + HY_BANDS:emb].set(bands)
    sel_np = np.full((1, hp), 3, np.int32)
    sel_np[0, 0] = 0
    sel_np[0, 1:1 + HY_BANDS] = 1
    sel_np[0, 1 + HY_BANDS:emb] = 2
    w_in = jnp.zeros((hp, hp), F32).at[:emb, :hid].set(fw_in)
    b_in = jnp.zeros((1, hp), F32).at[0, :hid].set(fb_in)
    w_mid = jnp.zeros((n_mid, hp, hp), F32).at[:, :hid, :hid].set(fw_mid)
    b_mid = jnp.zeros((n_mid, 1, hp), F32).at[:, 0, :hid].set(fb_mid)
    fr = jnp.zeros((1, hp), F32).at[0, :hid].set(freq)
    w_out = jnp.zeros((hp, 2 * width), F32).at[:hid].set(fw_out)
    deltas = jnp.abs(jnp.linspace(HY_MIN_DECAY, HY_MAX_DECAY, width, dtype=F32)).reshape(1, width)
    n_rows = 2 * length
    tr = min(256, length)
    half = length // tr
    const2 = lambda i: (0, 0)
    const3 = lambda i: (0, 0, 0)
    side = lambda i: (0, jnp.where(i < half, 0, 1))
    return pl.pallas_call(
        functools.partial(_filter_kernel, length=length, tr=tr),
        grid=(n_rows // tr,),
        in_specs=[
            pl.BlockSpec((1, hp), const2), pl.BlockSpec((1, hp), const2),
            pl.BlockSpec((hp, hp), const2), pl.BlockSpec((1, hp), const2),
            pl.BlockSpec((n_mid, hp, hp), const3), pl.BlockSpec((n_mid, 1, hp), const3),
            pl.BlockSpec((1, hp), const2),
            pl.BlockSpec((hp, width), side), pl.BlockSpec((1, width), side),
            pl.BlockSpec((1, width), const2),
        ],
        out_specs=pl.BlockSpec((tr, width), lambda i: (i, 0)),
        out_shape=jax.ShapeDtypeStruct((n_rows, width), F32),
        compiler_params=_params("parallel"),
    )(band_row, jnp.asarray(sel_np), w_in, b_in, w_mid, b_mid, fr, w_out,
      fb_out.reshape(1, 2 * width), deltas)


def _hi_lo(a):
    a = np.asarray(a, np.float32)
    hi = a.astype(BF16)
    lo = (a - hi.astype(np.float32)).astype(BF16)
    return hi, lo


def _rot(c, s, sign):
    return [[c, sign * s], [-sign * s, c]]


@functools.lru_cache(maxsize=None)
def _direct_tables(length):
    n = 2 * length
    th = 2 * np.pi * np.outer(np.arange(n), np.arange(n)) / n
    c, s = np.cos(th), np.sin(th)
    fwd = np.block(_rot(c[:, :length], s[:, :length], 1.0))
    flt = np.concatenate([c, -s], 0)
    inv = np.block(_rot(c[:length] / n, s[:length] / n, -1.0))
    return _hi_lo(fwd), _hi_lo(flt), _hi_lo(inv)


def _fft_split(n):
    n1 = 64 if n >= 8192 else 16
    return n1, n // n1


@functools.lru_cache(maxsize=None)
def _stage_tables(n1, n2):
    n = n1 * n2
    h1 = n1 // 2
    s_ = FFT_S
    k1 = np.arange(n1, dtype=np.float64)
    eye = np.eye(s_)

    def stage1(cols, sg):
        n2v = (np.arange(n2 // sg)[:, None] * sg + np.arange(sg)[None, :])[:, None, :, None]
        th = 2 * np.pi * (k1[None, :, None, None] * np.arange(cols)[None, None, None, :] / n1
                          + n2v * k1[None, :, None, None] / n)
        return np.cos(th), np.sin(th)

    def spread(blocks, rows, cols, sg):
        nc, ncp = len(blocks), len(blocks[0])
        out = np.zeros((n2 // sg, nc, rows, sg, ncp, cols, sg))
        for ci in range(nc):
            for cj in range(ncp):
                out[:, ci, :, :, cj, :, :] = np.einsum("jksm,st->jksmt", blocks[ci][cj], np.eye(sg))
        return out.reshape(n2 // sg, nc * rows * sg, ncp * cols * sg)

    c, s = stage1(h1, s_)
    m1 = spread(_rot(c, s, 1.0), n1, h1, s_)
    c, s = stage1(n1, FFT_SF)
    m1f = spread([[c], [-s]], n1, n1, FFT_SF)
    ph = 2 * np.pi * np.outer(np.arange(n2), np.arange(n2)) / n2
    t2 = np.block(_rot(np.cos(ph), np.sin(ph), 1.0))
    ps = 2 * np.pi * (np.outer(np.arange(n2), np.arange(n2))[None] / n2
                      + k1[:, None, None] * np.arange(n2)[None, :, None] / n)
    c, s = np.cos(ps), np.sin(ps)
    t2i = np.concatenate([np.concatenate([c, -s], 2), np.concatenate([s, c], 2)], 1)
    th = 2 * np.pi * np.outer(np.arange(h1), k1) / n1
    c, s = np.cos(th) / n, np.sin(th) / n
    blk = _rot(c, s, -1.0)
    m3 = np.zeros((2, h1, s_, 2, n1, s_))
    for ci in range(2):
        for cj in range(2):
            m3[ci, :, :, cj, :, :] = np.einsum("nk,st->nskt", blk[ci][cj], eye)
    m3 = m3.reshape(2 * h1 * s_, 2 * n1 * s_)
    return _hi_lo(m1), _hi_lo(m1f), _hi_lo(t2), _hi_lo(t2i), _hi_lo(m3)


def _split_bf16(x):
    hi = x.astype(BF16)
    return hi, (x - hi.astype(F32)).astype(BF16)


def _dot3(mh, ml, x):
    xh, xl = _split_bf16(x)
    dot = lambda a, b: jnp.dot(a, b, preferred_element_type=F32)
    return dot(mh, xh) + dot(mh, xl) + dot(ml, xh)


def _dot1(mh, x):
    return jnp.dot(mh, x.astype(BF16), preferred_element_type=F32)


def _tabmm_kernel(*refs, passes):
    x_ref, o_ref = refs[0], refs[-1]
    x = x_ref[...].reshape(-1, x_ref.shape[-1])
    y = _dot1(refs[1][...], x) if passes == 1 else _dot3(refs[1][...], refs[2][...], x)
    o_ref[...] = y.reshape(o_ref.shape).astype(o_ref.dtype)


def _tabmm(x, x_spec, tab, tab_spec, out_shape, out_spec, grid, passes, out_dtype=F32):
    tabs = [jnp.asarray(tab[0])] + ([jnp.asarray(tab[1])] if passes == 3 else [])
    return pl.pallas_call(
        functools.partial(_tabmm_kernel, passes=passes),
        grid=grid,
        in_specs=[x_spec] + [tab_spec] * len(tabs),
        out_specs=out_spec,
        out_shape=jax.ShapeDtypeStruct(out_shape, out_dtype),
        compiler_params=_params(*(["parallel"] * len(grid))),
    )(x, *tabs)


def _cmul_rows(s, kf_ref):
    n = s.shape[0] // 2
    sr, si = s[:n], s[n:]
    kr, ki = kf_ref[0], kf_ref[1]
    return jnp.concatenate([sr * kr - si * ki, sr * ki + si * kr], axis=0)


def _fft_mid_kernel(x_ref, kf_ref, t_ref, i_ref, o_ref):
    x = x_ref[...].reshape(-1, x_ref.shape[-1])
    y = _cmul_rows(_dot1(t_ref[...], x), kf_ref)
    o_ref[...] = _dot1(i_ref[...], y).reshape(o_ref.shape).astype(o_ref.dtype)


def _fft_s3_kernel(b_ref, m_ref, v_ref, g_ref, sk_ref, z_ref):
    x = b_ref[...].reshape(-1, b_ref.shape[-1])
    y = _dot1(m_ref[...], x).reshape(z_ref.shape)
    z_ref[...] = (g_ref[...] * (y + v_ref[...] * sk_ref[...])).astype(z_ref.dtype)


def _direct_conv_kernel(v_ref, g_ref, kf_ref, f_ref, i_ref, sk_ref, z_ref):
    x = v_ref[...].reshape(-1, v_ref.shape[-1])
    y = _cmul_rows(_dot1(f_ref[...], x), kf_ref)
    o = _dot1(i_ref[...], y).reshape(z_ref.shape)
    z_ref[...] = (g_ref[...] * (o + v_ref[...] * sk_ref[...])).astype(z_ref.dtype)


def _long_conv_direct(vprime, xg, kfull, skip, bsz, length, width):
    n = 2 * length
    wt = min(WT_FFT // 2, width)
    nw = width // wt
    fwd, flt, inv = _direct_tables(length)
    kf = _tabmm(kfull, pl.BlockSpec((n, wt), lambda j: (0, j)), flt,
                pl.BlockSpec((2 * n, n), lambda j: (0, 0)),
                (2, n, width), pl.BlockSpec((2, n, wt), lambda j: (0, 0, j)), (nw,), 3)
    tok = pl.BlockSpec((2, length, wt), lambda j, p: (p, 0, j))
    const = lambda shape: pl.BlockSpec(shape, lambda j, p: (0, 0))
    z = pl.pallas_call(
        _direct_conv_kernel,
        grid=(nw, bsz // 2),
        in_specs=[tok, tok, pl.BlockSpec((2, n, wt), lambda j, p: (0, 0, j)),
                  const((2 * n, n)), const((n, 2 * n)),
                  pl.BlockSpec((1, wt), lambda j, p: (0, j))],
        out_specs=tok,
        out_shape=jax.ShapeDtypeStruct((bsz, length, width), BF16),
        compiler_params=_params("parallel", "parallel"),
    )(vprime.reshape(bsz, length, width), xg.reshape(bsz, length, width), kf,
      jnp.asarray(fwd[0]), jnp.asarray(inv[0]), skip.reshape(1, width))
    return z.reshape(bsz * length, width)


def _long_conv_staged(vprime, xg, kfull, skip, bsz, length, width):
    n = 2 * length
    n1, n2 = _fft_split(n)
    h1 = n1 // 2
    s_ = FFT_S
    nj = n2 // s_
    wt = min(WT_FFT, width)
    nw = width // wt
    pairs = bsz // 2
    m1, m1f, t2, t2i, m3 = _stage_tables(n1, n2)
    spec5 = lambda sg: pl.BlockSpec((None, 2, n1, sg, wt), lambda j, w, p: (p, 0, 0, j, w))
    tab_j = lambda tab: pl.BlockSpec((None,) + tab[0].shape[1:], lambda j, w, p: (j, 0, 0))
    sf = FFT_SF
    ka = _tabmm(kfull.reshape(n1, n2, width), pl.BlockSpec((n1, sf, wt), lambda j, w, p: (0, j, w)),
                m1f, tab_j(m1f), (1, 2, n1, n2, width), spec5(sf), (n2 // sf, nw, 1), 3)
    kf = _tabmm(ka, pl.BlockSpec((None, 2, None, n2, wt), lambda k, w: (0, 0, k, 0, w)), t2,
                pl.BlockSpec((2 * n2, 2 * n2), lambda k, w: (0, 0)),
                (2, n1, n2, width), pl.BlockSpec((2, None, n2, wt), lambda k, w: (0, k, 0, w)), (n1, nw), 3)
    v4 = vprime.reshape(bsz, h1, n2, width)
    g4 = xg.reshape(bsz, h1, n2, width)
    tok = pl.BlockSpec((2, h1, s_, wt), lambda j, w, p: (p, 0, j, w))
    a = _tabmm(v4, tok, m1, tab_j(m1), (pairs, 2, n1, n2, width), spec5(s_), (nj, nw, pairs), 1, BF16)
    wm = min(2 * WT_FFT, width)
    blk = pl.BlockSpec((None, 2, None, n2, wm), lambda k, w, p: (p, 0, k, 0, w))
    sq = 2 * n2
    bm = pl.pallas_call(
        _fft_mid_kernel,
        grid=(n1, width // wm, pairs),
        in_specs=[blk, pl.BlockSpec((2, None, n2, wm), lambda k, w, p: (0, k, 0, w)),
                  pl.BlockSpec((sq, sq), lambda k, w, p: (0, 0)),
                  pl.BlockSpec((None, sq, sq), lambda k, w, p: (k, 0, 0))],
        out_specs=blk,
        out_shape=jax.ShapeDtypeStruct((pairs, 2, n1, n2, width), BF16),
        compiler_params=_params("parallel", "parallel", "parallel"),
    )(a, kf, jnp.asarray(t2[0]), jnp.asarray(t2i[0]))
    m3_spec = pl.BlockSpec(m3[0].shape, lambda j, w, p: (0, 0))
    z = pl.pallas_call(
        _fft_s3_kernel,
        grid=(nj, nw, pairs),
        in_specs=[spec5(s_), m3_spec, tok, tok,
                  pl.BlockSpec((1, wt), lambda j, w, p: (0, w))],
        out_specs=tok,
        out_shape=jax.ShapeDtypeStruct((bsz, h1, n2, width), BF16),
        compiler_params=_params("parallel", "parallel", "parallel"),
    )(bm, jnp.asarray(m3[0]), v4, g4, skip.reshape(1, width))
    return z.reshape(bsz * length, width)


def _hyena_core(vprime, xg, bsz, length, hy):
    fw_in, fb_in, fw_mid, fb_mid, freq, fw_out, fb_out, skip = hy
    width = vprime.shape[1]
    kfull = _hyena_filter(length, width, fw_in, fb_in, fw_mid, fb_mid, freq, fw_out, fb_out)
    conv = _long_conv_direct if length <= FFT_DIRECT_MAX else _long_conv_staged
    return conv(vprime, xg, kfull, skip, bsz, length, width)


def _gla_levels(cc):
    lv = []
    m = cc // 2
    while m >= GLA_DIAG:
        lv.append(m)
        m //= 2
    return lv


def _gla_level_map(cc, reverse):
    t = np.arange(cc)[:, None]
    s = np.arange(cc)[None, :]
    out = np.full((cc, cc), -1, np.int32)
    for idx, m in enumerate(_gla_levels(cc)):
        same = (t // (2 * m)) == (s // (2 * m))
        t_hi = (t % (2 * m)) >= m
        s_hi = (s % (2 * m)) >= m
        own = same & (t_hi & ~s_hi if not reverse else ~t_hi & s_hi)
        out[own] = idx
    return out


def _gla_scan_kernel(*refs, reverse, has_init, keep_state, finish, scale, cc, dk, dv):
    refs = list(refs)
    q_ref, k_ref, v_ref, low_ref, wg_ref, bg_ref, lv_ref, tri_ref = refs[:8]
    pos = 8
    if has_init:
        s0_ref = refs[pos]
        pos += 1
    if finish:
        of_ref, r_ref, ng_ref = refs[pos:pos + 3]
        pos += 3
    if keep_state and reverse:
        pos += 1
    o_ref = refs[pos]
    sf_ref = refs[pos + 1] if keep_state else None
    st_ref, b_ref = refs[-2:]
    c = pl.program_id(2)
    heads = range(q_ref.shape[1] // dk)
    ksl = lambda h: slice(h * dk, (h + 1) * dk)
    vsl = lambda h: slice(h * dv, (h + 1) * dv)

    @pl.when(c == 0)
    def _():
        for h in heads:
            st_ref[h] = s0_ref[h].T if has_init else jnp.zeros((dv, dk), F32)

    logit = jnp.dot(low_ref[...].astype(BF16), wg_ref[...], preferred_element_type=F32) + bg_ref[...]
    g = (jnp.minimum(logit, 0.0) - jnp.log(1.0 + jnp.exp(-jnp.abs(logit)))) * (LOG2_E / GLA_TAU)

    row = lax.broadcasted_iota(jnp.int32, g.shape, 0)
    g_hi = g.astype(BF16)
    rest = g - g_hi.astype(F32)
    g_mid = rest.astype(BF16)
    g_lo = (rest - g_mid.astype(F32)).astype(BF16)
    b = jnp.dot(tri_ref[...], jnp.concatenate([g_hi, g_mid, g_lo], axis=0), preferred_element_type=F32)
    b_ref[...] = b
    q = q_ref[...] * scale
    kk = k_ref[...]
    vb = v_ref[...].astype(BF16)

    edge = 0 if reverse else cc - 1
    b_edge = b_ref[edge:edge + 1, :]
    qe = (q * jnp.exp2(b)).astype(BF16)
    kd = (kk * jnp.exp2(b_edge - b)).astype(BF16)
    carry = jnp.exp2(b_edge)
    out = []
    for h in heads:
        state = st_ref[h]
        out.append(lax.dot_general(qe[:, ksl(h)], state.astype(BF16), (((1,), (1,)), ((), ())),
                                   preferred_element_type=F32))
        upd = lax.dot_general(vb[:, vsl(h)], kd[:, ksl(h)], (((0,), (0,)), ((), ())),
                              preferred_element_type=F32)
        st_ref[h] = state * carry[:, ksl(h)] + upd

    lvm = lv_ref[...]
    attn = [jnp.zeros((cc, cc), F32) for _ in heads]
    for idx, m in enumerate(_gla_levels(cc)):
        pieces = []
        for blk in range(cc // (2 * m)):
            ref_row = blk * 2 * m + (m if reverse else m - 1)
            pieces.append(jnp.broadcast_to(b_ref[ref_row:ref_row + 1, :], (2 * m, b.shape[1])))
        rr = pieces[0] if len(pieces) == 1 else jnp.concatenate(pieces, axis=0)
        later = (row & m) != 0
        qside = jnp.logical_not(later) if reverse else later
        w = jnp.exp2(jnp.minimum(jnp.where(qside, b - rr, rr - b), 0.0))
        z = (jnp.where(qside, q, kk) * w).astype(BF16)
        for h in heads:
            zh = z[:, ksl(h)]
            p = lax.dot_general(zh, zh, (((1,), (1,)), ((), ())), preferred_element_type=F32)
            attn[h] = jnp.where(lvm == idx, p, attn[h])

    ti = lax.broadcasted_iota(jnp.int32, (cc, cc), 0)
    si = lax.broadcasted_iota(jnp.int32, (cc, cc), 1)
    tin = ti & (GLA_DIAG - 1)
    diag = [jnp.zeros((cc, cc), F32) for _ in heads]
    for dlt in range(GLA_DIAG):
        if dlt == 0:
            term = q * kk
        else:
            shift = cc - dlt if reverse else dlt
            decay = jnp.exp2(jnp.minimum(b - pltpu.roll(b, shift, 0), 0.0))
            term = q * pltpu.roll(kk, shift, 0) * decay
        if reverse:
            own = (si == ti + dlt) & (tin < GLA_DIAG - dlt)
        else:
            own = (si == ti - dlt) & (tin >= dlt)
        for h in heads:
            diag[h] = jnp.where(own, jnp.sum(term[:, ksl(h)], axis=1, keepdims=True), diag[h])

    for h in heads:
        a = jnp.where(lvm < 0, diag[h], attn[h])
        o = out[h] + jnp.dot(a.astype(BF16), vb[:, vsl(h)], preferred_element_type=F32)
        if finish:
            o = o + of_ref[:, vsl(h)]
            o = o * lax.rsqrt(jnp.mean(o * o, axis=-1, keepdims=True) + GLA_NORM_EPS)
            o = o * ng_ref[...] * _silu(r_ref[:, vsl(h)])
        o_ref[:, vsl(h)] = o.astype(o_ref.dtype)

    if keep_state:
        @pl.when(c == pl.num_programs(2) - 1)
        def _():
            for h in heads:
                sf_ref[h] = st_ref[h].T


def _gla_scan(proj, low, wdir, bg, norm_g, init, o_first, states, bsz, length, dk, dv, direction, cc):
    reverse = direction == 1
    finish = o_first is not None
    keep_state = init is None
    nh = GLA_HEADS
    hps = GLA_HEADS_PER_STEP
    nc = length // cc
    t = bsz * length
    wk, wv = hps * dk, hps * dv
    kblk = (nh * dk) // wk
    vblk = (2 * nh * dk) // wv
    rblk = (2 * nh * dk + nh * dv) // wv
    assert nh % hps == 0 and (2 * nh * dk) % wv == 0
    lw = low.shape[1]

    def chunk(b, c):
        return b * nc + (nc - 1 - c if reverse else c)

    head = pl.BlockSpec((cc, wv), lambda b, h, c: (chunk(b, c), h))
    in_specs = [
        pl.BlockSpec((cc, wk), lambda b, h, c: (chunk(b, c), h)),
        pl.BlockSpec((cc, wk), lambda b, h, c: (chunk(b, c), kblk + h)),
        pl.BlockSpec((cc, wv), lambda b, h, c: (chunk(b, c), vblk + h)),
        pl.BlockSpec((cc, lw), lambda b, h, c: (chunk(b, c), 0)),
        pl.BlockSpec((None, lw, wk), lambda b, h, c: (direction, 0, h)),
        pl.BlockSpec((None, 1, wk), lambda b, h, c: (direction, 0, h)),
        pl.BlockSpec((cc, cc), lambda b, h, c: (0, 0)),
        pl.BlockSpec((cc, 3 * cc), lambda b, h, c: (0, 0)),
    ]
    tri = np.triu(np.ones((cc, cc), np.float32)) if reverse else np.tril(np.ones((cc, cc), np.float32))
    ins = [proj, proj, proj, low, wdir, bg, jnp.asarray(_gla_level_map(cc, reverse)),
           jnp.asarray(np.tile(tri, (1, 3)), dtype=BF16)]
    if init is not None:
        state, layer = init
        in_specs.append(pl.BlockSpec((None, None, None, hps, dk, dv),
                                     lambda b, h, c: (b, layer, direction, h, 0, 0)))
        ins.append(state)
    if finish:
        in_specs += [head, pl.BlockSpec((cc, wv), lambda b, h, c: (chunk(b, c), rblk + h)),
                     pl.BlockSpec((1, dv), lambda b, h, c: (0, 0))]
        ins += [o_first, proj, norm_g.reshape(1, dv)]
    out_specs = [head]
    out_shape = [jax.ShapeDtypeStruct((t, nh * dv), BF16 if finish else F32)]
    aliases = {}
    if keep_state:
        out_specs.append(pl.BlockSpec((None, None, hps, dk, dv), lambda b, h, c: (b, direction, h, 0, 0)))
        out_shape.append(jax.ShapeDtypeStruct((bsz, 2, nh, dk, dv), F32))
        if states is not None:
            in_specs.append(pl.BlockSpec(memory_space=pl.ANY))
            ins.append(states)
            aliases = {len(ins) - 1: 1}
    res = pl.pallas_call(
        functools.partial(_gla_scan_kernel, reverse=reverse, has_init=init is not None,
                          keep_state=keep_state, finish=finish, scale=dk ** -0.5, cc=cc, dk=dk, dv=dv),
        grid=(bsz, nh // hps, nc),
        in_specs=in_specs,
        out_specs=out_specs,
        out_shape=out_shape,
        input_output_aliases=aliases,
        scratch_shapes=[pltpu.VMEM((hps, dv, dk), F32), pltpu.VMEM((cc, wk), F32)],
        compiler_params=_params("parallel", "parallel", "arbitrary"),
    )(*ins)
    return (res[0], res[1]) if keep_state else (res[0], None)


def _gla_core(u, proj, bsz, length, init, gla):
    wg1, wg2, bg, norm_g = gla
    d = u.shape[1]
    rank = wg1.shape[2]
    dk_tot = wg2.shape[2]
    dk = dk_tot // GLA_HEADS
    dv = norm_g.shape[0]
    wlow = jnp.zeros((d, LANES), F32).at[:, :rank].set(wg1[0]).at[:, rank:2 * rank].set(wg1[1])
    low = _matmul(u, wlow.astype(BF16))
    wdir = jnp.zeros((2, LANES, dk_tot), F32).at[0, :rank].set(wg2[0]).at[1, rank:2 * rank].set(wg2[1])
    wdir = wdir.astype(BF16)
    cc = min(GLA_CHUNK, length)
    o, states = None, None
    for direction in (0, 1):
        o, states = _gla_scan(proj, low, wdir, bg.reshape(2, 1, dk_tot), norm_g, init, o, states,
                              bsz, length, dk, dv, direction, cc)
    return o, states


def kernel(x_prompt, x_sample, state_gla, c, c_ctx, ada_w, ada_b, ln_g, ln_b,
           hy_w_in, hy_conv_w, hy_conv_b, hy_fw_in, hy_fb_in, hy_fw_mid, hy_fb_mid,
           hy_freq, hy_fw_out, hy_fb_out, hy_skip, hy_w_out,
           sc_w_in, sc_conv_w, sc_w_out,
           gla_w_in, gla_wg1, gla_wg2, gla_bg, gla_norm_g, gla_w_out):
    depth, d, _ = ada_w.shape
    alpha = (2 * depth) ** 0.25
    bp, lp, _ = x_prompt.shape
    bs, ls, _ = x_sample.shape
    assert 1 + bs <= MOD_ROWS

    cond = jnp.zeros((MOD_ROWS, d), F32).at[0].set(c_ctx).at[1:1 + bs].set(c)
    mods = _modulation(cond, ada_w, ada_b)
    mods = mods.reshape(depth, MOD_ROWS, 3, 1, d).transpose(0, 2, 1, 3, 4)

    streams = [
        dict(x=x_prompt, b=bp, l=lp, period=lp, row0=0, stride=0, tt=min(TT, lp)),
        dict(x=x_sample, b=bs, l=ls, period=min(GRID_W, ls), row0=1, stride=1, tt=min(TT, ls)),
    ]
    for s in streams:
        s["u"] = _premod(s["x"], mods[0, 0], mods[0, 1], s["row0"], s["stride"], s["tt"])

    new_states = []
    for i in range(depth):
        kind = i % N_MIXERS
        j = i // N_MIXERS
        if kind == 0:
            w_in, w_out = hy_w_in[j].astype(BF16), hy_w_out[j].astype(BF16)
            hy = (hy_fw_in[j], hy_fb_in[j], hy_fw_mid[j], hy_fb_mid[j],
                  hy_freq[j], hy_fw_out[j], hy_fb_out[j], hy_skip[j])
        elif kind == 1:
            w_in, w_out = sc_w_in[j].astype(BF16), sc_w_out[j].astype(BF16)
        else:
            w_in, w_out = gla_w_in[j].astype(BF16), gla_w_out[j].astype(BF16)
            gla = (gla_wg1[j], gla_wg2[j], gla_bg[j], gla_norm_g[j])
        for si, s in enumerate(streams):
            u2 = s["u"].reshape(s["b"] * s["l"], d)
            if kind == 0:
                vprime, xg = _inproj_fused(u2, w_in, "hyena", hy_conv_w[j], hy_conv_b[j], s["period"])
                z = _hyena_core(vprime, xg, s["b"], s["l"], hy)
            elif kind == 1:
                z = _inproj_fused(u2, w_in, "sconv", sc_conv_w[j], None, s["period"])
            else:
                init = None if si == 0 else (state_gla, j)
                z, finals = _gla_core(u2, _matmul(u2, w_in), s["b"], s["l"], init, gla)
                if si == 0:
                    new_states.append(finals.astype(x_prompt.dtype))
            h = _matmul(z, w_out, BF16)
            nxt = None if i + 1 == depth else (mods[i + 1, 0], mods[i + 1, 1])
            s["x"], s["u"] = _ln_step(s["x"], h, mods[i, 2], ln_g[i], ln_b[i], nxt, alpha,
                                      s["row0"], s["stride"], s["tt"])
    return streams[0]["x"], streams[1]["x"], jnp.stack(new_states, axis=1)
```

```python
import functools
import math

import numpy as np
import jax
import jax.numpy as jnp
from jax import lax
from jax.experimental import pallas as pl
from jax.experimental.pallas import tpu as pltpu

GRID_W = 64
N_MIXERS = 3
LN_EPS = 1e-5
HY_BANDS = 16
HY_TARGET = 1e-2
HY_FAST_PCT = 0.3
HY_SLOW_PCT = 1.5
HY_MIN_DECAY = math.log(HY_TARGET) / HY_SLOW_PCT
HY_MAX_DECAY = math.log(HY_TARGET) / HY_FAST_PCT
GLA_HEADS = 4
GLA_TAU = 16.0
GLA_NORM_EPS = 1e-6
LOG2_E = math.log2(math.e)

LANES = 128
SUBLANES = 8
MOD_ROWS = 16
VMEM_LIMIT = 56 * 1024 * 1024

TM = 1024
TN = 1024
CT_FUSED = 256
TT = 256
WT_FFT = 1024
FFT_S = 2 * SUBLANES
FFT_SF = SUBLANES
FFT_DIRECT_MAX = 512
GLA_CHUNK = 256
GLA_HEADS_PER_STEP = 2
GLA_DIAG = 4

F32 = jnp.float32
BF16 = jnp.bfloat16
HIGHEST = lax.Precision.HIGHEST


def _params(*sem):
    return pltpu.CompilerParams(dimension_semantics=sem, vmem_limit_bytes=VMEM_LIMIT)


def _silu(x):
    return x / (1.0 + jnp.exp(-x))


def _mod_kernel(c_ref, w_ref, b_ref, o_ref):
    s = _silu(c_ref[...])
    o_ref[...] = jnp.dot(s, w_ref[...], precision=HIGHEST, preferred_element_type=F32) + b_ref[...]


def _modulation(cond, ada_w, ada_b):
    depth, d, n = ada_w.shape
    tn = 512
    return pl.pallas_call(
        _mod_kernel,
        grid=(depth, n // tn),
        in_specs=[
            pl.BlockSpec((MOD_ROWS, d), lambda i, j: (0, 0)),
            pl.BlockSpec((None, d, tn), lambda i, j: (i, 0, j)),
            pl.BlockSpec((None, 1, tn), lambda i, j: (i, 0, j)),
        ],
        out_specs=pl.BlockSpec((None, MOD_ROWS, tn), lambda i, j: (i, 0, j)),
        out_shape=jax.ShapeDtypeStruct((depth, MOD_ROWS, n), F32),
        compiler_params=_params("parallel", "parallel"),
    )(cond, ada_w, ada_b.reshape(depth, 1, n))


def _premod_kernel(x_ref, sh_ref, sc_ref, u_ref):
    u_ref[...] = (x_ref[...] * (1.0 + sc_ref[...]) + sh_ref[...]).astype(BF16)


def _row_spec(d, row0, stride):
    return pl.BlockSpec((None, 1, d), lambda b, t: (row0 + stride * b, 0, 0))


def _premod(x, shift, scale, row0, stride, tt):
    bsz, length, d = x.shape
    tok = pl.BlockSpec((None, tt, d), lambda b, t: (b, t, 0))
    return pl.pallas_call(
        _premod_kernel,
        grid=(bsz, length // tt),
        in_specs=[tok, _row_spec(d, row0, stride), _row_spec(d, row0, stride)],
        out_specs=tok,
        out_shape=jax.ShapeDtypeStruct(x.shape, BF16),
        compiler_params=_params("parallel", "parallel"),
    )(x, shift, scale)


def _mm_kernel(a_ref, b_ref, o_ref):
    o_ref[...] = jnp.dot(a_ref[...].astype(BF16), b_ref[...], preferred_element_type=F32).astype(o_ref.dtype)


def _matmul(a, b, out_dtype=F32):
    m, k = a.shape
    n = b.shape[1]
    tm = min(TM if a.dtype == BF16 else TM // 2, m)
    tn = min(TN, n)
    assert m % tm == 0 and n % tn == 0
    return pl.pallas_call(
        _mm_kernel,
        grid=(n // tn, m // tm),
        in_specs=[
            pl.BlockSpec((tm, k), lambda j, i: (i, 0)),
            pl.BlockSpec((k, tn), lambda j, i: (0, j)),
        ],
        out_specs=pl.BlockSpec((tm, tn), lambda j, i: (i, j)),
        out_shape=jax.ShapeDtypeStruct((m, n), out_dtype),
        compiler_params=_params("parallel", "parallel"),
    )(a, b)


def _conv3_rows(x, w_ref, period):
    rows = x.shape[0]
    pos = lax.broadcasted_iota(jnp.int32, x.shape, 0) % period
    prev = jnp.where(pos == 0, 0.0, pltpu.roll(x, 1, 0))
    nxt = jnp.where(pos == period - 1, 0.0, pltpu.roll(x, rows - 1, 0))
    return prev * w_ref[0:1, :] + x * w_ref[1:2, :] + nxt * w_ref[2:3, :]


def _inproj_sconv_kernel(a_ref, wb_ref, wc_ref, wx_ref, wg_ref, cw_ref, z_ref, *, period):
    a = a_ref[...]
    dot = lambda w_ref: jnp.dot(a, w_ref[...], preferred_element_type=F32)
    y = dot(wb_ref) * _conv3_rows(dot(wc_ref) * dot(wx_ref), cw_ref, period)
    z_ref[...] = (y * _silu(dot(wg_ref))).astype(BF16)


def _inproj_hyena_kernel(a_ref, w0_ref, w1_ref, w2_ref, wg_ref, c0_ref, c1_ref, c2_ref,
                         b0_ref, b1_ref, b2_ref, v_ref, xg_ref, *, period):
    a = a_ref[...]
    dot = lambda w_ref: jnp.dot(a, w_ref[...], preferred_element_type=F32)
    x1 = _conv3_rows(dot(w1_ref), c1_ref, period) + b1_ref[...]
    v = _conv3_rows(dot(w2_ref), c2_ref, period) + b2_ref[...]
    v_ref[...] = v * x1
    x0 = _conv3_rows(dot(w0_ref), c0_ref, period) + b0_ref[...]
    xg_ref[...] = x0 * _silu(dot(wg_ref))


def _inproj_fused(u, w_in, kind, conv_w, conv_b, period):
    t, d = u.shape
    w = w_in.shape[1] // 4
    ct = min(CT_FUSED, w)
    tm = min(TM, t)
    nb = w // ct
    assert t % tm == 0 and tm % period == 0

    def col(rows, k):
        return pl.BlockSpec((rows, ct), lambda j, i, k=k: (0, k * nb + j))

    a_spec = pl.BlockSpec((tm, d), lambda j, i: (i, 0))
    out = pl.BlockSpec((tm, ct), lambda j, i: (i, j))
    w_specs = [col(d, k) for k in range(4)]
    if kind == "sconv":
        return pl.pallas_call(
            functools.partial(_inproj_sconv_kernel, period=period),
            grid=(nb, t // tm),
            in_specs=[a_spec] + w_specs + [col(3, 0)],
            out_specs=out,
            out_shape=jax.ShapeDtypeStruct((t, w), BF16),
            compiler_params=_params("parallel", "parallel"),
        )(u, w_in, w_in, w_in, w_in, conv_w)
    cb = conv_b.reshape(1, 3 * w)
    return pl.pallas_call(
        functools.partial(_inproj_hyena_kernel, period=period),
        grid=(nb, t // tm),
        in_specs=[a_spec] + w_specs + [col(3, k) for k in range(3)] + [col(1, k) for k in range(3)],
        out_specs=[out, out],
        out_shape=[jax.ShapeDtypeStruct((t, w), F32)] * 2,
        compiler_params=_params("parallel", "parallel"),
    )(u, w_in, w_in, w_in, w_in, conv_w, conv_w, conv_w, cb, cb, cb)


def _ln_kernel(*refs, alpha, with_next):
    if with_next:
        x_ref, h_ref, gt_ref, g_ref, b_ref, sh_ref, sc_ref, xo_ref, uo_ref = refs
    else:
        x_ref, h_ref, gt_ref, g_ref, b_ref, xo_ref = refs
    y = alpha * x_ref[...] + gt_ref[...] * h_ref[...]
    mu = jnp.mean(y, axis=-1, keepdims=True)
    dlt = y - mu
    var = jnp.mean(dlt * dlt, axis=-1, keepdims=True)
    xn = dlt * lax.rsqrt(var + LN_EPS) * g_ref[...] + b_ref[...]
    xo_ref[...] = xn
    if with_next:
        uo_ref[...] = (xn * (1.0 + sc_ref[...]) + sh_ref[...]).astype(BF16)


def _ln_step(x, h, gate, ln_g, ln_b, nxt, alpha, row0, stride, tt):
    bsz, length, d = x.shape
    tok = pl.BlockSpec((None, tt, d), lambda b, t: (b, t, 0))
    row = _row_spec(d, row0, stride)
    vec = pl.BlockSpec((1, d), lambda b, t: (0, 0))
    ins = [x, h.reshape(x.shape), gate, ln_g.reshape(1, d), ln_b.reshape(1, d)]
    specs = [tok, tok, row, vec, vec]
    out_shape = [jax.ShapeDtypeStruct(x.shape, F32)]
    out_specs = [tok]
    if nxt is not None:
        ins += [nxt[0], nxt[1]]
        specs += [row, row]
        out_shape.append(jax.ShapeDtypeStruct(x.shape, BF16))
        out_specs.append(tok)
    res = pl.pallas_call(
        functools.partial(_ln_kernel, alpha=alpha, with_next=nxt is not None),
        grid=(bsz, length // tt),
        in_specs=specs,
        out_specs=out_specs,
        out_shape=out_shape,
        compiler_params=_params("parallel", "parallel"),
    )(*ins)
    return (res[0], res[1]) if nxt is not None else (res[0], None)


def _filter_kernel(band_ref, sel_ref, w_in_ref, b_in_ref, w_mid_ref, b_mid_ref, fr_ref,
                   w_out_ref, b_out_ref, dl_ref, k_ref, *, length, tr):
    n = pl.program_id(0) * tr + lax.broadcasted_iota(jnp.int32, (tr, 1), 0)
    pos = jnp.where(n < length, n, 2 * length - n)
    t = pos.astype(F32) / length
    ang = (2.0 * math.pi) * t * band_ref[...]
    sel = sel_ref[...]
    z = jnp.where(sel == 0, t, jnp.where(sel == 1, jnp.cos(ang), jnp.where(sel == 2, -jnp.sin(ang), 0.0)))
    fr = fr_ref[...]
    h = jnp.sin(fr * (jnp.dot(z, w_in_ref[...], precision=HIGHEST, preferred_element_type=F32) + b_in_ref[...]))
    for m in range(w_mid_ref.shape[0]):
        h = jnp.sin(fr * (jnp.dot(h, w_mid_ref[m], precision=HIGHEST, preferred_element_type=F32)
                          + b_mid_ref[m]))
    o = jnp.dot(h, w_out_ref[...], precision=HIGHEST, preferred_element_type=F32) + b_out_ref[...]
    o = o * jnp.exp(-t * dl_ref[...])
    k_ref[...] = jnp.where(n == length, 0.0, o)


def _hyena_filter(length, width, fw_in, fb_in, fw_mid, fb_mid, freq, fw_out, fb_out):
    hid = fw_in.shape[1]
    n_mid = fw_mid.shape[0]
    emb = fw_in.shape[0]
    hp = LANES
    bands = jnp.linspace(1e-4, HY_BANDS - 1, HY_BANDS, dtype=F32)
    band_row = jnp.zeros((1, hp), F32).at[0, 1:1 + HY_BANDS].set(bands).at[0, 1 + HY_BANDS:emb].set(bands)
    sel_np = np.full((1, hp), 3, np.int32)
    sel_np[0, 0] = 0
    sel_np[0, 1:1 + HY_BANDS] = 1
    sel_np[0, 1 + HY_BANDS:emb] = 2
    w_in = jnp.zeros((hp, hp), F32).at[:emb, :hid].set(fw_in)
    b_in = jnp.zeros((1, hp), F32).at[0, :hid].set(fb_in)
    w_mid = jnp.zeros((n_mid, hp, hp), F32).at[:, :hid, :hid].set(fw_mid)
    b_mid = jnp.zeros((n_mid, 1, hp), F32).at[:, 0, :hid].set(fb_mid)
    fr = jnp.zeros((1, hp), F32).at[0, :hid].set(freq)
    w_out = jnp.zeros((hp, 2 * width), F32).at[:hid].set(fw_out)
    deltas = jnp.abs(jnp.linspace(HY_MIN_DECAY, HY_MAX_DECAY, width, dtype=F32)).reshape(1, width)
    n_rows = 2 * length
    tr = min(256, length)
    half = length // tr
    const2 = lambda i: (0, 0)
    const3 = lambda i: (0, 0, 0)
    side = lambda i: (0, jnp.where(i < half, 0, 1))
    return pl.pallas_call(
        functools.partial(_filter_kernel, length=length, tr=tr),
        grid=(n_rows // tr,),
        in_specs=[
            pl.BlockSpec((1, hp), const2), pl.BlockSpec((1, hp), const2),
            pl.BlockSpec((hp, hp), const2), pl.BlockSpec((1, hp), const2),
            pl.BlockSpec((n_mid, hp, hp), const3), pl.BlockSpec((n_mid, 1, hp), const3),
            pl.BlockSpec((1, hp), const2),
            pl.BlockSpec((hp, width), side), pl.BlockSpec((1, width), side),
            pl.BlockSpec((1, width), const2),
        ],
        out_specs=pl.BlockSpec((tr, width), lambda i: (i, 0)),
        out_shape=jax.ShapeDtypeStruct((n_rows, width), F32),
        compiler_params=_params("parallel"),
    )(band_row, jnp.asarray(sel_np), w_in, b_in, w_mid, b_mid, fr, w_out,
      fb_out.reshape(1, 2 * width), deltas)


def _hi_lo(a):
    a = np.asarray(a, np.float32)
    hi = a.astype(BF16)
    lo = (a - hi.astype(np.float32)).astype(BF16)
    return hi, lo


def _rot(c, s, sign):
    return [[c, sign * s], [-sign * s, c]]


@functools.lru_cache(maxsize=None)
def _direct_tables(length):
    n = 2 * length
    th = 2 * np.pi * np.outer(np.arange(n), np.arange(n)) / n
    c, s = np.cos(th), np.sin(th)
    fwd = np.block(_rot(c[:, :length], s[:, :length], 1.0))
    flt = np.concatenate([c, -s], 0)
    inv = np.block(_rot(c[:length] / n, s[:length] / n, -1.0))
    return _hi_lo(fwd), _hi_lo(flt), _hi_lo(inv)


def _fft_split(n):
    n1 = 64 if n >= 8192 else 16
    return n1, n // n1


@functools.lru_cache(maxsize=None)
def _stage_tables(n1, n2):
    n = n1 * n2
    h1 = n1 // 2
    s_ = FFT_S
    k1 = np.arange(n1, dtype=np.float64)
    eye = np.eye(s_)

    def stage1(cols, sg):
        n2v = (np.arange(n2 // sg)[:, None] * sg + np.arange(sg)[None, :])[:, None, :, None]
        th = 2 * np.pi * (k1[None, :, None, None] * np.arange(cols)[None, None, None, :] / n1
                          + n2v * k1[None, :, None, None] / n)
        return np.cos(th), np.sin(th)

    def spread(blocks, rows, cols, sg):
        nc, ncp = len(blocks), len(blocks[0])
        out = np.zeros((n2 // sg, nc, rows, sg, ncp, cols, sg))
        for ci in range(nc):
            for cj in range(ncp):
                out[:, ci, :, :, cj, :, :] = np.einsum("jksm,st->jksmt", blocks[ci][cj], np.eye(sg))
        return out.reshape(n2 // sg, nc * rows * sg, ncp * cols * sg)

    c, s = stage1(h1, s_)
    m1 = spread(_rot(c, s, 1.0), n1, h1, s_)
    c, s = stage1(n1, FFT_SF)
    m1f = spread([[c], [-s]], n1, n1, FFT_SF)
    ph = 2 * np.pi * np.outer(np.arange(n2), np.arange(n2)) / n2
    t2 = np.block(_rot(np.cos(ph), np.sin(ph), 1.0))
    ps = 2 * np.pi * (np.outer(np.arange(n2), np.arange(n2))[None] / n2
                      + k1[:, None, None] * np.arange(n2)[None, :, None] / n)
    c, s = np.cos(ps), np.sin(ps)
    t2i = np.concatenate([np.concatenate([c, -s], 2), np.concatenate([s, c], 2)], 1)
    th = 2 * np.pi * np.outer(np.arange(h1), k1) / n1
    c, s = np.cos(th) / n, np.sin(th) / n
    blk = _rot(c, s, -1.0)
    m3 = np.zeros((2, h1, s_, 2, n1, s_))
    for ci in range(2):
        for cj in range(2):
            m3[ci, :, :, cj, :, :] = np.einsum("nk,st->nskt", blk[ci][cj], eye)
    m3 = m3.reshape(2 * h1 * s_, 2 * n1 * s_)
    return _hi_lo(m1), _hi_lo(m1f), _hi_lo(t2), _hi_lo(t2i), _hi_lo(m3)


def _split_bf16(x):
    hi = x.astype(BF16)
    return hi, (x - hi.astype(F32)).astype(BF16)


def _dot3(mh, ml, x):
    xh, xl = _split_bf16(x)
    dot = lambda a, b: jnp.dot(a, b, preferred_element_type=F32)
    return dot(mh, xh) + dot(mh, xl) + dot(ml, xh)


def _dot1(mh, x):
    return jnp.dot(mh, x.astype(BF16), preferred_element_type=F32)


def _tabmm_kernel(*refs, passes):
    x_ref, o_ref = refs[0], refs[-1]
    x = x_ref[...].reshape(-1, x_ref.shape[-1])
    y = _dot1(refs[1][...], x) if passes == 1 else _dot3(refs[1][...], refs[2][...], x)
    o_ref[...] = y.reshape(o_ref.shape).astype(o_ref.dtype)


def _tabmm(x, x_spec, tab, tab_spec, out_shape, out_spec, grid, passes, out_dtype=F32):
    tabs = [jnp.asarray(tab[0])] + ([jnp.asarray(tab[1])] if passes == 3 else [])
    return pl.pallas_call(
        functools.partial(_tabmm_kernel, passes=passes),
        grid=grid,
        in_specs=[x_spec] + [tab_spec] * len(tabs),
        out_specs=out_spec,
        out_shape=jax.ShapeDtypeStruct(out_shape, out_dtype),
        compiler_params=_params(*(["parallel"] * len(grid))),
    )(x, *tabs)


def _cmul_rows(s, kf_ref):
    n = s.shape[0] // 2
    sr, si = s[:n], s[n:]
    kr, ki = kf_ref[0], kf_ref[1]
    return jnp.concatenate([sr * kr - si * ki, sr * ki + si * kr], axis=0)


def _fft_mid_kernel(x_ref, kf_ref, t_ref, i_ref, o_ref):
    x = x_ref[...].reshape(-1, x_ref.shape[-1])
    y = _cmul_rows(_dot1(t_ref[...], x), kf_ref)
    o_ref[...] = _dot1(i_ref[...], y).reshape(o_ref.shape).astype(o_ref.dtype)


def _fft_s3_kernel(b_ref, m_ref, v_ref, g_ref, sk_ref, z_ref):
    x = b_ref[...].reshape(-1, b_ref.shape[-1])
    y = _dot1(m_ref[...], x).reshape(z_ref.shape)
    z_ref[...] = (g_ref[...] * (y + v_ref[...] * sk_ref[...])).astype(z_ref.dtype)


def _direct_conv_kernel(v_ref, g_ref, kf_ref, f_ref, i_ref, sk_ref, z_ref):
    x = v_ref[...].reshape(-1, v_ref.shape[-1])
    y = _cmul_rows(_dot1(f_ref[...], x), kf_ref)
    o = _dot1(i_ref[...], y).reshape(z_ref.shape)
    z_ref[...] = (g_ref[...] * (o + v_ref[...] * sk_ref[...])).astype(z_ref.dtype)


def _long_conv_direct(vprime, xg, kfull, skip, bsz, length, width):
    n = 2 * length
    wt = min(WT_FFT // 2, width)
    nw = width // wt
    fwd, flt, inv = _direct_tables(length)
    kf = _tabmm(kfull, pl.BlockSpec((n, wt), lambda j: (0, j)), flt,
                pl.BlockSpec((2 * n, n), lambda j: (0, 0)),
                (2, n, width), pl.BlockSpec((2, n, wt), lambda j: (0, 0, j)), (nw,), 3)
    tok = pl.BlockSpec((2, length, wt), lambda j, p: (p, 0, j))
    const = lambda shape: pl.BlockSpec(shape, lambda j, p: (0, 0))
    z = pl.pallas_call(
        _direct_conv_kernel,
        grid=(nw, bsz // 2),
        in_specs=[tok, tok, pl.BlockSpec((2, n, wt), lambda j, p: (0, 0, j)),
                  const((2 * n, n)), const((n, 2 * n)),
                  pl.BlockSpec((1, wt), lambda j, p: (0, j))],
        out_specs=tok,
        out_shape=jax.ShapeDtypeStruct((bsz, length, width), BF16),
        compiler_params=_params("parallel", "parallel"),
    )(vprime.reshape(bsz, length, width), xg.reshape(bsz, length, width), kf,
      jnp.asarray(fwd[0]), jnp.asarray(inv[0]), skip.reshape(1, width))
    return z.reshape(bsz * length, width)


def _long_conv_staged(vprime, xg, kfull, skip, bsz, length, width):
    n = 2 * length
    n1, n2 = _fft_split(n)
    h1 = n1 // 2
    s_ = FFT_S
    nj = n2 // s_
    wt = min(WT_FFT, width)
    nw = width // wt
    pairs = bsz // 2
    m1, m1f, t2, t2i, m3 = _stage_tables(n1, n2)
    spec5 = lambda sg: pl.BlockSpec((None, 2, n1, sg, wt), lambda j, w, p: (p, 0, 0, j, w))
    tab_j = lambda tab: pl.BlockSpec((None,) + tab[0].shape[1:], lambda j, w, p: (j, 0, 0))
    sf = FFT_SF
    ka = _tabmm(kfull.reshape(n1, n2, width), pl.BlockSpec((n1, sf, wt), lambda j, w, p: (0, j, w)),
                m1f, tab_j(m1f), (1, 2, n1, n2, width), spec5(sf), (n2 // sf, nw, 1), 1)
    wm = min(2 * WT_FFT, width)
    kf = _tabmm(ka, pl.BlockSpec((None, 2, None, n2, wm), lambda k, w: (0, 0, k, 0, w)), t2,
                pl.BlockSpec((2 * n2, 2 * n2), lambda k, w: (0, 0)),
                (2, n1, n2, width), pl.BlockSpec((2, None, n2, wm), lambda k, w: (0, k, 0, w)),
                (n1, width // wm), 1)
    v4 = vprime.reshape(bsz, h1, n2, width)
    g4 = xg.reshape(bsz, h1, n2, width)
    tok = pl.BlockSpec((2, h1, s_, wt), lambda j, w, p: (p, 0, j, w))
    a = _tabmm(v4, tok, m1, tab_j(m1), (pairs, 2, n1, n2, width), spec5(s_), (nj, nw, pairs), 1, BF16)
    blk =pl.BlockSpec((None, 2, None, n2, wm), lambda k, w, p: (p, 0, k, 0, w))
    sq = 2 * n2
    bm = pl.pallas_call(
        _fft_mid_kernel,
        grid=(n1, width // wm, pairs),
        in_specs=[blk, pl.BlockSpec((2, None, n2, wm), lambda k, w, p: (0, k, 0, w)),
                  pl.BlockSpec((sq, sq), lambda k, w, p: (0, 0)),
                  pl.BlockSpec((None, sq, sq), lambda k, w, p: (k, 0, 0))],
        out_specs=blk,
        out_shape=jax.ShapeDtypeStruct((pairs, 2, n1, n2, width), BF16),
        compiler_params=_params("parallel", "parallel", "parallel"),
    )(a, kf, jnp.asarray(t2[0]), jnp.asarray(t2i[0]))
    m3_spec = pl.BlockSpec(m3[0].shape, lambda j, w, p: (0, 0))
    z = pl.pallas_call(
        _fft_s3_kernel,
        grid=(nj, nw, pairs),
        in_specs=[spec5(s_), m3_spec, tok, tok,
                  pl.BlockSpec((1, wt), lambda j, w, p: (0, w))],
        out_specs=tok,
        out_shape=jax.ShapeDtypeStruct((bsz, h1, n2, width), BF16),
        compiler_params=_params("parallel", "parallel", "parallel"),
    )(bm, jnp.asarray(m3[0]), v4, g4, skip.reshape(1, width))
    return z.reshape(bsz * length, width)


def _hyena_core(vprime, xg, bsz, length, hy):
    fw_in, fb_in, fw_mid, fb_mid, freq, fw_out, fb_out, skip = hy
    width = vprime.shape[1]
    kfull = _hyena_filter(length, width, fw_in, fb_in, fw_mid, fb_mid, freq, fw_out, fb_out)
    conv = _long_conv_direct if length <= FFT_DIRECT_MAX else _long_conv_staged
    return conv(vprime, xg, kfull, skip, bsz, length, width)


def _gla_levels(cc):
    lv = []
    m = cc // 2
    while m >= GLA_DIAG:
        lv.append(m)
        m //= 2
    return lv


def _gla_level_map(cc, reverse):
    t = np.arange(cc)[:, None]
    s = np.arange(cc)[None, :]
    out = np.full((cc, cc), -1, np.int32)
    for idx, m in enumerate(_gla_levels(cc)):
        same = (t // (2 * m)) == (s // (2 * m))
        t_hi = (t % (2 * m)) >= m
        s_hi = (s % (2 * m)) >= m
        own = same & (t_hi & ~s_hi if not reverse else ~t_hi & s_hi)
        out[own] = idx
    return out


def _gla_scan_kernel(*refs, reverse, has_init, keep_state, finish, scale, cc, dk, dv):
    refs = list(refs)
    q_ref, k_ref, v_ref, low_ref, wg_ref, bg_ref, lv_ref, tri_ref = refs[:8]
    pos = 8
    if has_init:
        s0_ref = refs[pos]
        pos += 1
    if finish:
        of_ref, r_ref, ng_ref = refs[pos:pos + 3]
        pos += 3
    if keep_state and reverse:
        pos += 1
    o_ref = refs[pos]
    sf_ref = refs[pos + 1] if keep_state else None
    st_ref, b_ref = refs[-2:]
    c = pl.program_id(2)
    heads = range(q_ref.shape[1] // dk)
    ksl = lambda h: slice(h * dk, (h + 1) * dk)
    vsl = lambda h: slice(h * dv, (h + 1) * dv)

    @pl.when(c == 0)
    def _():
        for h in heads:
            st_ref[h] = s0_ref[h].T if has_init else jnp.zeros((dv, dk), F32)

    logit = jnp.dot(low_ref[...].astype(BF16), wg_ref[...], preferred_element_type=F32) + bg_ref[...]
    g = (jnp.minimum(logit, 0.0) - jnp.log(1.0 + jnp.exp(-jnp.abs(logit)))) * (LOG2_E / GLA_TAU)

    row = lax.broadcasted_iota(jnp.int32, g.shape, 0)
    g_hi = g.astype(BF16)
    rest = g - g_hi.astype(F32)
    g_mid = rest.astype(BF16)
    g_lo = (rest - g_mid.astype(F32)).astype(BF16)
    b = jnp.dot(tri_ref[...], jnp.concatenate([g_hi, g_mid, g_lo], axis=0), preferred_element_type=F32)
    b_ref[...] = b
    q = q_ref[...] * scale
    kk = k_ref[...]
    vb = v_ref[...].astype(BF16)

    edge = 0 if reverse else cc - 1
    b_edge = b_ref[edge:edge + 1, :]
    qe = (q * jnp.exp2(b)).astype(BF16)
    kd = (kk * jnp.exp2(b_edge - b)).astype(BF16)
    carry = jnp.exp2(b_edge)
    out = []
    for h in heads:
        state = st_ref[h]
        out.append(lax.dot_general(qe[:, ksl(h)], state.astype(BF16), (((1,), (1,)), ((), ())),
                                   preferred_element_type=F32))
        upd = lax.dot_general(vb[:, vsl(h)], kd[:, ksl(h)], (((0,), (0,)), ((), ())),
                              preferred_element_type=F32)
        st_ref[h] = state * carry[:, ksl(h)] + upd

    lvm = lv_ref[...]
    attn = [jnp.zeros((cc, cc), F32) for _ in heads]
    for idx, m in enumerate(_gla_levels(cc)):
        pieces = []
        for blk in range(cc // (2 * m)):
            ref_row = blk * 2 * m + (m if reverse else m - 1)
            pieces.append(jnp.broadcast_to(b_ref[ref_row:ref_row + 1, :], (2 * m, b.shape[1])))
        rr = pieces[0] if len(pieces) == 1 else jnp.concatenate(pieces, axis=0)
        later = (row & m) != 0
        qside = jnp.logical_not(later) if reverse else later
        w = jnp.exp2(jnp.minimum(jnp.where(qside, b - rr, rr - b), 0.0))
        z = (jnp.where(qside, q, kk) * w).astype(BF16)
        for h in heads:
            zh = z[:, ksl(h)]
            p = lax.dot_general(zh, zh, (((1,), (1,)), ((), ())), preferred_element_type=F32)
            attn[h] = jnp.where(lvm == idx, p, attn[h])

    ti = lax.broadcasted_iota(jnp.int32, (cc, cc), 0)
    si = lax.broadcasted_iota(jnp.int32, (cc, cc), 1)
    tin = ti & (GLA_DIAG - 1)
    diag = [jnp.zeros((cc, cc), F32) for _ in heads]
    for dlt in range(GLA_DIAG):
        if dlt == 0:
            term = q * kk
        else:
            shift = cc - dlt if reverse else dlt
            decay = jnp.exp2(jnp.minimum(b - pltpu.roll(b, shift, 0), 0.0))
            term = q * pltpu.roll(kk, shift, 0) * decay
        if reverse:
            own = (si == ti + dlt) & (tin < GLA_DIAG - dlt)
        else:
            own = (si == ti - dlt) & (tin >= dlt)
        for h in heads:
            diag[h] = jnp.where(own, jnp.sum(term[:, ksl(h)], axis=1, keepdims=True), diag[h])

    for h in heads:
        a = jnp.where(lvm < 0, diag[h], attn[h])
        o = out[h] + jnp.dot(a.astype(BF16), vb[:, vsl(h)], preferred_element_type=F32)
        if finish:
            o = o + of_ref[:, vsl(h)]
            o = o * lax.rsqrt(jnp.mean(o * o, axis=-1, keepdims=True) + GLA_NORM_EPS)
            o = o * ng_ref[...] * _silu(r_ref[:, vsl(h)])
        o_ref[:, vsl(h)] = o.astype(o_ref.dtype)

    if keep_state:
        @pl.when(c == pl.num_programs(2) - 1)
        def _():
            for h in heads:
                sf_ref[h] = st_ref[h].T


def _gla_scan(proj, low, wdir, bg, norm_g, init, o_first, states, bsz, length, dk, dv, direction, cc):
    reverse = direction == 1
    finish = o_first is not None
    keep_state = init is None
    nh = GLA_HEADS
    hps = GLA_HEADS_PER_STEP
    nc = length // cc
    t = bsz * length
    wk, wv = hps * dk, hps * dv
    kblk = (nh * dk) // wk
    vblk = (2 * nh * dk) // wv
    rblk = (2 * nh * dk + nh * dv) // wv
    assert nh % hps == 0 and (2 * nh * dk) % wv == 0
    lw = low.shape[1]

    def chunk(b, c):
        return b * nc + (nc - 1 - c if reverse else c)

    head = pl.BlockSpec((cc, wv), lambda b, h, c: (chunk(b, c), h))
    in_specs = [
        pl.BlockSpec((cc, wk), lambda b, h, c: (chunk(b, c), h)),
        pl.BlockSpec((cc, wk), lambda b, h, c: (chunk(b, c), kblk + h)),
        pl.BlockSpec((cc, wv), lambda b, h, c: (chunk(b, c), vblk + h)),
        pl.BlockSpec((cc, lw), lambda b, h, c: (chunk(b, c), 0)),
        pl.BlockSpec((None, lw, wk), lambda b, h, c: (direction, 0, h)),
        pl.BlockSpec((None, 1, wk), lambda b, h, c: (direction, 0, h)),
        pl.BlockSpec((cc, cc), lambda b, h, c: (0, 0)),
        pl.BlockSpec((cc, 3 * cc), lambda b, h, c: (0, 0)),
    ]
    tri = np.triu(np.ones((cc, cc), np.float32)) if reverse else np.tril(np.ones((cc, cc), np.float32))
    ins = [proj, proj, proj, low, wdir, bg, jnp.asarray(_gla_level_map(cc, reverse)),
           jnp.asarray(np.tile(tri, (1, 3)), dtype=BF16)]
    if init is not None:
        state, layer = init
        in_specs.append(pl.BlockSpec((None, None, None, hps, dk, dv),
                                     lambda b, h, c: (b, layer, direction, h, 0, 0)))
        ins.append(state)
    if finish:
        in_specs += [head, pl.BlockSpec((cc, wv), lambda b, h, c: (chunk(b, c), rblk + h)),
                     pl.BlockSpec((1, dv), lambda b, h, c: (0, 0))]
        ins += [o_first, proj, norm_g.reshape(1, dv)]
    out_specs = [head]
    out_shape = [jax.ShapeDtypeStruct((t, nh * dv), BF16 if finish else F32)]
    aliases = {}
    if keep_state:
        out_specs.append(pl.BlockSpec((None, None, hps, dk, dv), lambda b, h, c: (b, direction, h, 0, 0)))
        out_shape.append(jax.ShapeDtypeStruct((bsz, 2, nh, dk, dv), F32))
        if states is not None:
            in_specs.append(pl.BlockSpec(memory_space=pl.ANY))
            ins.append(states)
            aliases = {len(ins) - 1: 1}
    res = pl.pallas_call(
        functools.partial(_gla_scan_kernel, reverse=reverse, has_init=init is not None,
                          keep_state=keep_state, finish=finish, scale=dk ** -0.5, cc=cc, dk=dk, dv=dv),
        grid=(bsz, nh // hps, nc),
        in_specs=in_specs,
        out_specs=out_specs,
        out_shape=out_shape,
        input_output_aliases=aliases,
        scratch_shapes=[pltpu.VMEM((hps, dv, dk), F32), pltpu.VMEM((cc, wk), F32)],
        compiler_params=_params("parallel", "parallel", "arbitrary"),
    )(*ins)
    return (res[0], res[1]) if keep_state else (res[0], None)


def _gla_core(u, proj, bsz, length, init, gla):
    wg1, wg2, bg, norm_g = gla
    d = u.shape[1]
    rank = wg1.shape[2]
    dk_tot = wg2.shape[2]
    dk = dk_tot // GLA_HEADS
    dv = norm_g.shape[0]
    wlow = jnp.zeros((d, LANES), F32).at[:, :rank].set(wg1[0]).at[:, rank:2 * rank].set(wg1[1])
    low = _matmul(u, wlow.astype(BF16))
    wdir = jnp.zeros((2, LANES, dk_tot), F32).at[0, :rank].set(wg2[0]).at[1, rank:2 * rank].set(wg2[1])
    wdir = wdir.astype(BF16)
    cc = min(GLA_CHUNK, length)
    o, states = None, None
    for direction in (0, 1):
        o, states = _gla_scan(proj, low, wdir, bg.reshape(2, 1, dk_tot), norm_g, init, o, states,
                              bsz, length, dk, dv, direction, cc)
    return o, states


def kernel(x_prompt, x_sample, state_gla, c, c_ctx, ada_w, ada_b, ln_g, ln_b,
           hy_w_in, hy_conv_w, hy_conv_b, hy_fw_in, hy_fb_in, hy_fw_mid, hy_fb_mid,
           hy_freq, hy_fw_out, hy_fb_out, hy_skip, hy_w_out,
           sc_w_in, sc_conv_w, sc_w_out,
           gla_w_in, gla_wg1, gla_wg2, gla_bg, gla_norm_g, gla_w_out):
    depth, d, _ = ada_w.shape
    alpha = (2 * depth) ** 0.25
    bp, lp, _ = x_prompt.shape
    bs, ls, _ = x_sample.shape
    assert 1 + bs <= MOD_ROWS

    cond = jnp.zeros((MOD_ROWS, d), F32).at[0].set(c_ctx).at[1:1 + bs].set(c)
    mods = _modulation(cond, ada_w, ada_b)
    mods = mods.reshape(depth, MOD_ROWS, 3, 1, d).transpose(0, 2, 1, 3, 4)

    streams = [
        dict(x=x_prompt, b=bp, l=lp, period=lp, row0=0, stride=0, tt=min(TT, lp)),
        dict(x=x_sample, b=bs, l=ls, period=min(GRID_W, ls), row0=1, stride=1, tt=min(TT, ls)),
    ]
    for s in streams:
        s["u"] = _premod(s["x"], mods[0, 0], mods[0, 1], s["row0"], s["stride"], s["tt"])

    new_states = []
    for i in range(depth):
        kind = i % N_MIXERS
        j = i // N_MIXERS
        if kind == 0:
            w_in, w_out = hy_w_in[j].astype(BF16), hy_w_out[j].astype(BF16)
            hy = (hy_fw_in[j], hy_fb_in[j], hy_fw_mid[j], hy_fb_mid[j],
                  hy_freq[j], hy_fw_out[j], hy_fb_out[j], hy_skip[j])
        elif kind == 1:
            w_in, w_out = sc_w_in[j].astype(BF16), sc_w_out[j].astype(BF16)
        else:
            w_in, w_out = gla_w_in[j].astype(BF16), gla_w_out[j].astype(BF16)
            gla = (gla_wg1[j], gla_wg2[j], gla_bg[j], gla_norm_g[j])
        for si, s in enumerate(streams):
            u2 = s["u"].reshape(s["b"] * s["l"], d)
            if kind == 0:
                vprime, xg = _inproj_fused(u2, w_in, "hyena", hy_conv_w[j], hy_conv_b[j], s["period"])
                z = _hyena_core(vprime, xg, s["b"], s["l"], hy)
            elif kind == 1:
                z = _inproj_fused(u2, w_in, "sconv", sc_conv_w[j], None, s["period"])
            else:
                init = None if si == 0 else (state_gla, j)
                z, finals = _gla_core(u2, _matmul(u2, w_in), s["b"], s["l"], init, gla)
                if si == 0:
                    new_states.append(finals.astype(x_prompt.dtype))
            h = _matmul(z, w_out, BF16)
            nxt = None if i + 1 == depth else (mods[i + 1, 0], mods[i + 1, 1])
            s["x"], s["u"] = _ln_step(s["x"], h, mods[i, 2], ln_g[i], ln_b[i], nxt, alpha,
                                      s["row0"], s["stride"], s["tt"])
    return streams[0]["x"], streams[1]["x"], jnp.stack(new_states, axis=1)
```

```python
import functools
import math

import numpy as np
import jax
import jax.numpy as jnp
from jax import lax
from jax.experimental import pallas as pl
from jax.experimental.pallas import tpu as pltpu

GRID_W = 64
N_MIXERS = 3
LN_EPS = 1e-5
HY_BANDS = 16
HY_TARGET = 1e-2
HY_FAST_PCT = 0.3
HY_SLOW_PCT = 1.5
HY_MIN_DECAY = math.log(HY_TARGET) / HY_SLOW_PCT
HY_MAX_DECAY = math.log(HY_TARGET) / HY_FAST_PCT
GLA_HEADS = 4
GLA_TAU = 16.0
GLA_NORM_EPS = 1e-6
LOG2_E = math.log2(math.e)

LANES = 128
SUBLANES = 8
MOD_ROWS = 16
VMEM_LIMIT = 56 * 1024 * 1024

TM = 1024
TN = 1024
CT_FUSED = 256
TT = 256
TN_MOD = 512
TR_FILTER = 256
WT_FFT = 1024
FFT_S = 2 * SUBLANES
FFT_SF = SUBLANES
FFT_DIRECT_MAX = 512
GLA_CHUNK = 256
GLA_HEADS_PER_STEP = 2
GLA_DIAG = 4

F32 = jnp.float32
BF16 = jnp.bfloat16
HIGHEST = lax.Precision.HIGHEST


def _params(*sem):
    return pltpu.CompilerParams(dimension_semantics=sem, vmem_limit_bytes=VMEM_LIMIT)


def _silu(x):
    return x / (1.0 + jnp.exp(-x))


def _mod_kernel(c_ref, w_ref, b_ref, o_ref):
    s = _silu(c_ref[...])
    o_ref[...] = jnp.dot(s, w_ref[...], precision=HIGHEST, preferred_element_type=F32) + b_ref[...]


def _modulation(cond, ada_w, ada_b):
    depth, d, n = ada_w.shape
    tn = min(TN_MOD, n)
    return pl.pallas_call(
        _mod_kernel,
        grid=(depth, n // tn),
        in_specs=[
            pl.BlockSpec((MOD_ROWS, d), lambda i, j: (0, 0)),
            pl.BlockSpec((None, d, tn), lambda i, j: (i, 0, j)),
            pl.BlockSpec((None, 1, tn), lambda i, j: (i, 0, j)),
        ],
        out_specs=pl.BlockSpec((None, MOD_ROWS, tn), lambda i, j: (i, 0, j)),
        out_shape=jax.ShapeDtypeStruct((depth, MOD_ROWS, n), F32),
        compiler_params=_params("parallel", "parallel"),
    )(cond, ada_w, ada_b.reshape(depth, 1, n))


def _premod_kernel(x_ref, sh_ref, sc_ref, u_ref):
    u_ref[...] = (x_ref[...] * (1.0 + sc_ref[...]) + sh_ref[...]).astype(BF16)


def _row_spec(d, row0, stride):
    return pl.BlockSpec((None, 1, d), lambda b, t: (row0 + stride * b, 0, 0))


def _premod(x, shift, scale, row0, stride, tt):
    bsz, length, d = x.shape
    tok = pl.BlockSpec((None, tt, d), lambda b, t: (b, t, 0))
    return pl.pallas_call(
        _premod_kernel,
        grid=(bsz, length // tt),
        in_specs=[tok, _row_spec(d, row0, stride), _row_spec(d, row0, stride)],
        out_specs=tok,
        out_shape=jax.ShapeDtypeStruct(x.shape, BF16),
        compiler_params=_params("parallel", "parallel"),
    )(x, shift, scale)


def _mm_kernel(a_ref, b_ref, o_ref):
    o_ref[...] = jnp.dot(a_ref[...].astype(BF16), b_ref[...], preferred_element_type=F32).astype(o_ref.dtype)


def _matmul(a, b, out_dtype=F32):
    m, k = a.shape
    n = b.shape[1]
    tm = min(TM if a.dtype == BF16 else TM // 2, m)
    tn = min(TN, n)
    assert m % tm == 0 and n % tn == 0
    return pl.pallas_call(
        _mm_kernel,
        grid=(n // tn, m // tm),
        in_specs=[
            pl.BlockSpec((tm, k), lambda j, i: (i, 0)),
            pl.BlockSpec((k, tn), lambda j, i: (0, j)),
        ],
        out_specs=pl.BlockSpec((tm, tn), lambda j, i: (i, j)),
        out_shape=jax.ShapeDtypeStruct((m, n), out_dtype),
        compiler_params=_params("parallel", "parallel"),
    )(a, b)


def _conv3_rows(x, w_ref, period):
    rows = x.shape[0]
    pos = lax.broadcasted_iota(jnp.int32, x.shape, 0) % period
    prev = jnp.where(pos == 0, 0.0, pltpu.roll(x, 1, 0))
    nxt = jnp.where(pos == period - 1, 0.0, pltpu.roll(x, rows - 1, 0))
    return prev * w_ref[0:1, :] + x * w_ref[1:2, :] + nxt * w_ref[2:3, :]


def _inproj_sconv_kernel(a_ref, wb_ref, wc_ref, wx_ref, wg_ref, cw_ref, z_ref, *, period):
    a = a_ref[...]
    dot = lambda w_ref: jnp.dot(a, w_ref[...], preferred_element_type=F32)
    y = dot(wb_ref) * _conv3_rows(dot(wc_ref) * dot(wx_ref), cw_ref, period)
    z_ref[...] = (y * _silu(dot(wg_ref))).astype(BF16)


def _inproj_hyena_kernel(a_ref, w0_ref, w1_ref, w2_ref, wg_ref, c0_ref, c1_ref, c2_ref,
                         b0_ref, b1_ref, b2_ref, v_ref, xg_ref, *, period):
    a = a_ref[...]
    dot = lambda w_ref: jnp.dot(a, w_ref[...], preferred_element_type=F32)
    x1 = _conv3_rows(dot(w1_ref), c1_ref, period) + b1_ref[...]
    v = _conv3_rows(dot(w2_ref), c2_ref, period) + b2_ref[...]
    v_ref[...] = v * x1
    x0 = _conv3_rows(dot(w0_ref), c0_ref, period) + b0_ref[...]
    xg_ref[...] = x0 * _silu(dot(wg_ref))


def _inproj_fused(u, w_in, kind, conv_w, conv_b, period):
    t, d = u.shape
    w = w_in.shape[1] // 4
    ct = min(CT_FUSED, w)
    tm = min(TM, t)
    nb = w // ct
    assert t % tm == 0 and tm % period == 0

    def col(rows, k):
        return pl.BlockSpec((rows, ct), lambda j, i, k=k: (0, k * nb + j))

    a_spec = pl.BlockSpec((tm, d), lambda j, i: (i, 0))
    out = pl.BlockSpec((tm, ct), lambda j, i: (i, j))
    w_specs = [col(d, k) for k in range(4)]
    if kind == "sconv":
        return pl.pallas_call(
            functools.partial(_inproj_sconv_kernel, period=period),
            grid=(nb, t // tm),
            in_specs=[a_spec] + w_specs + [col(3, 0)],
            out_specs=out,
            out_shape=jax.ShapeDtypeStruct((t, w), BF16),
            compiler_params=_params("parallel", "parallel"),
        )(u, w_in, w_in, w_in, w_in, conv_w)
    cb = conv_b.reshape(1, 3 * w)
    return pl.pallas_call(
        functools.partial(_inproj_hyena_kernel, period=period),
        grid=(nb, t // tm),
        in_specs=[a_spec] + w_specs + [col(3, k) for k in range(3)] + [col(1, k) for k in range(3)],
        out_specs=[out, out],
        out_shape=[jax.ShapeDtypeStruct((t, w), F32)] * 2,
        compiler_params=_params("parallel", "parallel"),
    )(u, w_in, w_in, w_in, w_in, conv_w, conv_w, conv_w, cb, cb, cb)


def _ln_kernel(*refs, alpha, with_next):
    if with_next:
        x_ref, h_ref, gt_ref, g_ref, b_ref, sh_ref, sc_ref, xo_ref, uo_ref = refs
    else:
        x_ref, h_ref, gt_ref, g_ref, b_ref, xo_ref = refs
    y = alpha * x_ref[...] + gt_ref[...] * h_ref[...]
    mu = jnp.mean(y, axis=-1, keepdims=True)
    dlt = y - mu
    var = jnp.mean(dlt * dlt, axis=-1, keepdims=True)
    xn = dlt * lax.rsqrt(var + LN_EPS) * g_ref[...] + b_ref[...]
    xo_ref[...] = xn
    if with_next:
        uo_ref[...] = (xn * (1.0 + sc_ref[...]) + sh_ref[...]).astype(BF16)


def _ln_step(x, h, gate, ln_g, ln_b, nxt, alpha, row0, stride, tt):
    bsz, length, d = x.shape
    tok = pl.BlockSpec((None, tt, d), lambda b, t: (b, t, 0))
    row = _row_spec(d, row0, stride)
    vec = pl.BlockSpec((1, d), lambda b, t: (0, 0))
    ins = [x, h.reshape(x.shape), gate, ln_g.reshape(1, d), ln_b.reshape(1, d)]
    specs = [tok, tok, row, vec, vec]
    out_shape = [jax.ShapeDtypeStruct(x.shape, F32)]
    out_specs = [tok]
    if nxt is not None:
        ins += [nxt[0], nxt[1]]
        specs += [row, row]
        out_shape.append(jax.ShapeDtypeStruct(x.shape, BF16))
        out_specs.append(tok)
    res = pl.pallas_call(
        functools.partial(_ln_kernel, alpha=alpha, with_next=nxt is not None),
        grid=(bsz, length // tt),
        in_specs=specs,
        out_specs=out_specs,
        out_shape=out_shape,
        compiler_params=_params("parallel", "parallel"),
    )(*ins)
    return (res[0], res[1]) if nxt is not None else (res[0], None)


def _filter_kernel(band_ref, sel_ref, w_in_ref, b_in_ref, w_mid_ref, b_mid_ref, fr_ref,
                   w_out_ref, b_out_ref, dl_ref, k_ref, *, length, tr):
    n = pl.program_id(0) * tr + lax.broadcasted_iota(jnp.int32, (tr, 1), 0)
    pos = jnp.where(n < length, n, 2 * length - n)
    t = pos.astype(F32) / length
    ang = (2.0 * math.pi) * t * band_ref[...]
    sel = sel_ref[...]
    z = jnp.where(sel == 0, t, jnp.where(sel == 1, jnp.cos(ang), jnp.where(sel == 2, -jnp.sin(ang), 0.0)))
    fr = fr_ref[...]
    h = jnp.sin(fr * (jnp.dot(z, w_in_ref[...], precision=HIGHEST, preferred_element_type=F32) + b_in_ref[...]))
    for m in range(w_mid_ref.shape[0]):
        h = jnp.sin(fr * (jnp.dot(h, w_mid_ref[m], precision=HIGHEST, preferred_element_type=F32)
                          + b_mid_ref[m]))
    o = jnp.dot(h, w_out_ref[...], precision=HIGHEST, preferred_element_type=F32) + b_out_ref[...]
    o = o * jnp.exp(-t * dl_ref[...])
    k_ref[...] = jnp.where(n == length, 0.0, o)


def _hyena_filter(length, width, fw_in, fb_in, fw_mid, fb_mid, freq, fw_out, fb_out):
    hid = fw_in.shape[1]
    n_mid = fw_mid.shape[0]
    emb = fw_in.shape[0]
    hp = LANES
    bands = jnp.linspace(1e-4, HY_BANDS - 1, HY_BANDS, dtype=F32)
    band_row = jnp.zeros((1, hp), F32).at[0, 1:1 + HY_BANDS].set(bands).at[0, 1 + HY_BANDS:emb].set(bands)
    sel_np = np.full((1, hp), 3, np.int32)
    sel_np[0, 0] = 0
    sel_np[0, 1:1 + HY_BANDS] = 1
    sel_np[0, 1 + HY_BANDS:emb] = 2
    w_in = jnp.zeros((hp, hp), F32).at[:emb, :hid].set(fw_in)
    b_in = jnp.zeros((1, hp), F32).at[0, :hid].set(fb_in)
    w_mid = jnp.zeros((n_mid, hp, hp), F32).at[:, :hid, :hid].set(fw_mid)
    b_mid = jnp.zeros((n_mid, 1, hp), F32).at[:, 0, :hid].set(fb_mid)
    fr = jnp.zeros((1, hp), F32).at[0, :hid].set(freq)
    w_out = jnp.zeros((hp, 2 * width), F32).at[:hid].set(fw_out)
    deltas = jnp.abs(jnp.linspace(HY_MIN_DECAY, HY_MAX_DECAY, width, dtype=F32)).reshape(1, width)
    n_rows = 2 * length
    tr = min(TR_FILTER, length)
    half = length // tr
    const2 = lambda i: (0, 0)
    const3 = lambda i: (0, 0, 0)
    side = lambda i: (0, jnp.where(i < half, 0, 1))
    return pl.pallas_call(
        functools.partial(_filter_kernel, length=length, tr=tr),
        grid=(n_rows // tr,),
        in_specs=[
            pl.BlockSpec((1, hp), const2), pl.BlockSpec((1, hp), const2),
            pl.BlockSpec((hp, hp), const2), pl.BlockSpec((1, hp), const2),
            pl.BlockSpec((n_mid, hp, hp), const3), pl.BlockSpec((n_mid, 1, hp), const3),
            pl.BlockSpec((1, hp), const2),
            pl.BlockSpec((hp, width), side), pl.BlockSpec((1, width), side),
            pl.BlockSpec((1, width), const2),
        ],
        out_specs=pl.BlockSpec((tr, width), lambda i: (i, 0)),
        out_shape=jax.ShapeDtypeStruct((n_rows, width), F32),
        compiler_params=_params("parallel"),
    )(band_row, jnp.asarray(sel_np), w_in, b_in, w_mid, b_mid, fr, w_out,
      fb_out.reshape(1, 2 * width), deltas)


def _hi_lo(a):
    a = np.asarray(a, np.float32)
    hi = a.astype(BF16)
    lo = (a - hi.astype(np.float32)).astype(BF16)
    return hi, lo


def _rot(c, s, sign):
    return [[c, sign * s], [-sign * s, c]]


@functools.lru_cache(maxsize=None)
def _direct_tables(length):
    n = 2 * length
    th = 2 * np.pi * np.outer(np.arange(n), np.arange(n)) / n
    c, s = np.cos(th), np.sin(th)
    fwd = np.block(_rot(c[:, :length], s[:, :length], 1.0))
    flt = np.concatenate([c, -s], 0)
    inv = np.block(_rot(c[:length] / n, s[:length] / n, -1.0))
    return _hi_lo(fwd), _hi_lo(flt), _hi_lo(inv)


def _fft_split(n):
    n1 = 64 if n >= 8192 else 16
    return n1, n // n1


@functools.lru_cache(maxsize=None)
def _stage_tables(n1, n2):
    n = n1 * n2
    h1 = n1 // 2
    s_ = FFT_S
    k1 = np.arange(n1, dtype=np.float64)
    eye = np.eye(s_)

    def stage1(cols, sg):
        n2v = (np.arange(n2 // sg)[:, None] * sg + np.arange(sg)[None, :])[:, None, :, None]
        th = 2 * np.pi * (k1[None, :, None, None] * np.arange(cols)[None, None, None, :] / n1
                          + n2v * k1[None, :, None, None] / n)
        return np.cos(th), np.sin(th)

    def spread(blocks, rows, cols, sg):
        nc, ncp = len(blocks), len(blocks[0])
        out = np.zeros((n2 // sg, nc, rows, sg, ncp, cols, sg))
        for ci in range(nc):
            for cj in range(ncp):
                out[:, ci, :, :, cj, :, :] = np.einsum("jksm,st->jksmt", blocks[ci][cj], np.eye(sg))
        return out.reshape(n2 // sg, nc * rows * sg, ncp * cols * sg)

    c, s = stage1(h1, s_)
    m1 = spread(_rot(c, s, 1.0), n1, h1, s_)
    c, s = stage1(n1, FFT_SF)
    m1f = spread([[c], [-s]], n1, n1, FFT_SF)
    ph = 2 * np.pi * np.outer(np.arange(n2), np.arange(n2)) / n2
    t2 = np.block(_rot(np.cos(ph), np.sin(ph), 1.0))
    ps = 2 * np.pi * (np.outer(np.arange(n2), np.arange(n2))[None] / n2
                      + k1[:, None, None] * np.arange(n2)[None, :, None] / n)
    c, s = np.cos(ps), np.sin(ps)
    t2i = np.concatenate([np.concatenate([c, -s], 2), np.concatenate([s, c], 2)], 1)
    th = 2 * np.pi * np.outer(np.arange(h1), k1) / n1
    c, s = np.cos(th) / n, np.sin(th) / n
    blk = _rot(c, s, -1.0)
    m3 = np.zeros((2, h1, s_, 2, n1, s_))
    for ci in range(2):
        for cj in range(2):
            m3[ci, :, :, cj, :, :] = np.einsum("nk,st->nskt", blk[ci][cj], eye)
    m3 = m3.reshape(2 * h1 * s_, 2 * n1 * s_)
    return _hi_lo(m1), _hi_lo(m1f), _hi_lo(t2), _hi_lo(t2i), _hi_lo(m3)


def _split_bf16(x):
    hi = x.astype(BF16)
    return hi, (x - hi.astype(F32)).astype(BF16)


def _dot3(mh, ml, x):
    xh, xl = _split_bf16(x)
    dot = lambda a, b: jnp.dot(a, b, preferred_element_type=F32)
    return dot(mh, xh) + dot(mh, xl) + dot(ml, xh)


def _dot1(mh, x):
    return jnp.dot(mh, x.astype(BF16), preferred_element_type=F32)


def _tabmm_kernel(*refs, passes):
    x_ref, o_ref = refs[0], refs[-1]
    x = x_ref[...].reshape(-1, x_ref.shape[-1])
    y = _dot1(refs[1][...], x) if passes == 1 else _dot3(refs[1][...], refs[2][...], x)
    o_ref[...] = y.reshape(o_ref.shape).astype(o_ref.dtype)


def _tabmm(x, x_spec, tab, tab_spec, out_shape, out_spec, grid, passes, out_dtype=F32):
    tabs = [jnp.asarray(tab[0])] + ([jnp.asarray(tab[1])] if passes == 3 else [])
    return pl.pallas_call(
        functools.partial(_tabmm_kernel, passes=passes),
        grid=grid,
        in_specs=[x_spec] + [tab_spec] * len(tabs),
        out_specs=out_spec,
        out_shape=jax.ShapeDtypeStruct(out_shape, out_dtype),
        compiler_params=_params(*(["parallel"] * len(grid))),
    )(x, *tabs)


def _cmul_rows(s, kf_ref):
    n = s.shape[0] // 2
    sr, si = s[:n], s[n:]
    kr, ki = kf_ref[0], kf_ref[1]
    return jnp.concatenate([sr * kr - si * ki, sr * ki + si * kr], axis=0)


def _fft_mid_kernel(x_ref, kf_ref, t_ref, i_ref, o_ref):
    x = x_ref[...].reshape(-1, x_ref.shape[-1])
    y = _cmul_rows(_dot1(t_ref[...], x), kf_ref)
    o_ref[...] = _dot1(i_ref[...], y).reshape(o_ref.shape).astype(o_ref.dtype)


def _fft_s3_kernel(b_ref, m_ref, v_ref, g_ref, sk_ref, z_ref):
    x = b_ref[...].reshape(-1, b_ref.shape[-1])
    y = _dot1(m_ref[...], x).reshape(z_ref.shape)
    z_ref[...] = (g_ref[...] * (y + v_ref[...] * sk_ref[...])).astype(z_ref.dtype)


def _direct_conv_kernel(v_ref, g_ref, kf_ref, f_ref, i_ref, sk_ref, z_ref):
    x = v_ref[...].reshape(-1, v_ref.shape[-1])
    y = _cmul_rows(_dot1(f_ref[...], x), kf_ref)
    o = _dot1(i_ref[...], y).reshape(z_ref.shape)
    z_ref[...] = (g_ref[...] * (o + v_ref[...] * sk_ref[...])).astype(z_ref.dtype)


def _long_conv_direct(vprime, xg, kfull, skip, bsz, length, width):
    n = 2 * length
    wt = min(WT_FFT // 2, width)
    nw = width // wt
    fwd, flt, inv = _direct_tables(length)
    kf = _tabmm(kfull, pl.BlockSpec((n, wt), lambda j: (0, j)), flt,
                pl.BlockSpec((2 * n, n), lambda j: (0, 0)),
                (2, n, width), pl.BlockSpec((2, n, wt), lambda j: (0, 0, j)), (nw,), 3)
    tok = pl.BlockSpec((2, length, wt), lambda j, p: (p, 0, j))
    const = lambda shape: pl.BlockSpec(shape, lambda j, p: (0, 0))
    z = pl.pallas_call(
        _direct_conv_kernel,
        grid=(nw, bsz // 2),
        in_specs=[tok, tok, pl.BlockSpec((2, n, wt), lambda j, p: (0, 0, j)),
                  const((2 * n, n)), const((n, 2 * n)),
                  pl.BlockSpec((1, wt), lambda j, p: (0, j))],
        out_specs=tok,
        out_shape=jax.ShapeDtypeStruct((bsz, length, width), BF16),
        compiler_params=_params("parallel", "parallel"),
    )(vprime.reshape(bsz, length, width), xg.reshape(bsz, length, width), kf,
      jnp.asarray(fwd[0]), jnp.asarray(inv[0]), skip.reshape(1, width))
    return z.reshape(bsz * length, width)


def _long_conv_staged(vprime, xg, kfull, skip, bsz, length, width):
    n = 2 * length
    n1, n2 = _fft_split(n)
    h1 = n1 // 2
    s_ = FFT_S
    nj = n2 // s_
    wt = min(WT_FFT, width)
    nw = width // wt
    pairs = bsz // 2
    m1, m1f, t2, t2i, m3 = _stage_tables(n1, n2)
    spec5 = lambda sg: pl.BlockSpec((None, 2, n1, sg, wt), lambda j, w, p: (p, 0, 0, j, w))
    tab_j = lambda tab: pl.BlockSpec((None,) + tab[0].shape[1:], lambda j, w, p: (j, 0, 0))
    sf = FFT_SF
    ka = _tabmm(kfull.reshape(n1, n2, width), pl.BlockSpec((n1, sf, wt), lambda j, w, p: (0, j, w)),
                m1f, tab_j(m1f), (1, 2, n1, n2, width), spec5(sf), (n2 // sf, nw, 1), 1)
    wm = min(4 * WT_FFT, width)
    kf = _tabmm(ka, pl.BlockSpec((None, 2, None, n2, wm), lambda k, w: (0, 0, k, 0, w)), t2,
                pl.BlockSpec((2 * n2, 2 * n2), lambda k, w: (0, 0)),
                (2, n1, n2, width), pl.BlockSpec((2, None, n2, wm), lambda k, w: (0, k, 0, w)),
                (n1, width // wm), 1)
    v4 = vprime.reshape(bsz, h1, n2, width)
    g4 = xg.reshape(bsz, h1, n2, width)
    tok = pl.BlockSpec((2, h1, s_, wt), lambda j, w, p: (p, 0, j, w))
    a = _tabmm(v4, tok, m1, tab_j(m1), (pairs, 2, n1, n2, width), spec5(s_), (nj, nw, pairs), 1, BF16)
    blk =pl.BlockSpec((None, 2, None, n2, wm), lambda k, w, p: (p, 0, k, 0, w))
    sq = 2 * n2
    bm = pl.pallas_call(
        _fft_mid_kernel,
        grid=(n1, width // wm, pairs),
        in_specs=[blk, pl.BlockSpec((2, None, n2, wm), lambda k, w, p: (0, k, 0, w)),
                  pl.BlockSpec((sq, sq), lambda k, w, p: (0, 0)),
                  pl.BlockSpec((None, sq, sq), lambda k, w, p: (k, 0, 0))],
        out_specs=blk,
        out_shape=jax.ShapeDtypeStruct((pairs, 2, n1, n2, width), BF16),
        compiler_params=_params("parallel", "parallel", "parallel"),
    )(a, kf, jnp.asarray(t2[0]), jnp.asarray(t2i[0]))
    m3_spec = pl.BlockSpec(m3[0].shape, lambda j, w, p: (0, 0))
    z = pl.pallas_call(
        _fft_s3_kernel,
        grid=(nj, nw, pairs),
        in_specs=[spec5(s_), m3_spec, tok, tok,
                  pl.BlockSpec((1, wt), lambda j, w, p: (0, w))],
        out_specs=tok,
        out_shape=jax.ShapeDtypeStruct((bsz, h1, n2, width), BF16),
        compiler_params=_params("parallel", "parallel", "parallel"),
    )(bm, jnp.asarray(m3[0]), v4, g4, skip.reshape(1, width))
    return z.reshape(bsz * length, width)


def _hyena_core(vprime, xg, bsz, length, hy):
    fw_in, fb_in, fw_mid, fb_mid, freq, fw_out, fb_out, skip = hy
    width = vprime.shape[1]
    kfull = _hyena_filter(length, width, fw_in, fb_in, fw_mid, fb_mid, freq, fw_out, fb_out)
    conv = _long_conv_direct if length <= FFT_DIRECT_MAX else _long_conv_staged
    return conv(vprime, xg, kfull, skip, bsz, length, width)


def _gla_levels(cc):
    lv = []
    m = cc // 2
    while m >= GLA_DIAG:
        lv.append(m)
        m //= 2
    return lv


def _gla_level_map(cc, reverse):
    t = np.arange(cc)[:, None]
    s = np.arange(cc)[None, :]
    out = np.full((cc, cc), -1, np.int32)
    for idx, m in enumerate(_gla_levels(cc)):
        same = (t // (2 * m)) == (s // (2 * m))
        t_hi = (t % (2 * m)) >= m
        s_hi = (s % (2 * m)) >= m
        own = same & (t_hi & ~s_hi if not reverse else ~t_hi & s_hi)
        out[own] = idx
    return out


def _gla_scan_kernel(*refs, reverse, has_init, keep_state, finish, scale, cc, dk, dv):
    refs = list(refs)
    q_ref, k_ref, v_ref, low_ref, wg_ref, bg_ref, lv_ref, tri_ref = refs[:8]
    pos = 8
    if has_init:
        s0_ref = refs[pos]
        pos += 1
    if finish:
        of_ref, r_ref, ng_ref = refs[pos:pos + 3]
        pos += 3
    if keep_state and reverse:
        pos += 1
    o_ref = refs[pos]
    sf_ref = refs[pos + 1] if keep_state else None
    st_ref, b_ref = refs[-2:]
    c = pl.program_id(2)
    heads = range(q_ref.shape[1] // dk)
    ksl = lambda h: slice(h * dk, (h + 1) * dk)
    vsl = lambda h: slice(h * dv, (h + 1) * dv)

    @pl.when(c == 0)
    def _():
        for h in heads:
            st_ref[h] = s0_ref[h].T if has_init else jnp.zeros((dv, dk), F32)

    logit = jnp.dot(low_ref[...].astype(BF16), wg_ref[...], preferred_element_type=F32) + bg_ref[...]
    g = (jnp.minimum(logit, 0.0) - jnp.log(1.0 + jnp.exp(-jnp.abs(logit)))) * (LOG2_E / GLA_TAU)

    row = lax.broadcasted_iota(jnp.int32, g.shape, 0)
    g_hi = g.astype(BF16)
    rest = g - g_hi.astype(F32)
    g_mid = rest.astype(BF16)
    g_lo = (rest - g_mid.astype(F32)).astype(BF16)
    b = jnp.dot(tri_ref[...], jnp.concatenate([g_hi, g_mid, g_lo], axis=0), preferred_element_type=F32)
    b_ref[...] = b
    q = q_ref[...] * scale
    kk = k_ref[...]
    vb = v_ref[...].astype(BF16)

    edge = 0 if reverse else cc - 1
    b_edge = b_ref[edge:edge + 1, :]
    qe = (q * jnp.exp2(b)).astype(BF16)
    kd = (kk * jnp.exp2(b_edge - b)).astype(BF16)
    carry = jnp.exp2(b_edge)
    out = []
    for h in heads:
        state = st_ref[h]
        out.append(lax.dot_general(qe[:, ksl(h)], state.astype(BF16), (((1,), (1,)), ((), ())),
                                   preferred_element_type=F32))
        upd = lax.dot_general(vb[:, vsl(h)], kd[:, ksl(h)], (((0,), (0,)), ((), ())),
                              preferred_element_type=F32)
        st_ref[h] = state * carry[:, ksl(h)] + upd

    lvm = lv_ref[...]
    attn = [jnp.zeros((cc, cc), F32) for _ in heads]
    for idx, m in enumerate(_gla_levels(cc)):
        pieces = []
        for blk in range(cc // (2 * m)):
            ref_row = blk * 2 * m + (m if reverse else m - 1)
            pieces.append(jnp.broadcast_to(b_ref[ref_row:ref_row + 1, :], (2 * m, b.shape[1])))
        rr = pieces[0] if len(pieces) == 1 else jnp.concatenate(pieces, axis=0)
        later = (row & m) != 0
        qside = jnp.logical_not(later) if reverse else later
        w = jnp.exp2(jnp.minimum(jnp.where(qside, b - rr, rr - b), 0.0))
        z = (jnp.where(qside, q, kk) * w).astype(BF16)
        for h in heads:
            zh = z[:, ksl(h)]
            p = lax.dot_general(zh, zh, (((1,), (1,)), ((), ())), preferred_element_type=F32)
            attn[h] = jnp.where(lvm == idx, p, attn[h])

    ti = lax.broadcasted_iota(jnp.int32, (cc, cc), 0)
    si = lax.broadcasted_iota(jnp.int32, (cc, cc), 1)
    tin = ti & (GLA_DIAG - 1)
    diag = [jnp.zeros((cc, cc), F32) for _ in heads]
    for dlt in range(GLA_DIAG):
        if dlt == 0:
            term = q * kk
        else:
            shift = cc - dlt if reverse else dlt
            decay = jnp.exp2(jnp.minimum(b - pltpu.roll(b, shift, 0), 0.0))
            term = q * pltpu.roll(kk, shift, 0) * decay
        if reverse:
            own = (si == ti + dlt) & (tin < GLA_DIAG - dlt)
        else:
            own = (si == ti - dlt) & (tin >= dlt)
        for h in heads:
            diag[h] = jnp.where(own, jnp.sum(term[:, ksl(h)], axis=1, keepdims=True), diag[h])

    for h in heads:
        a = jnp.where(lvm < 0, diag[h], attn[h])
        o = out[h] + jnp.dot(a.astype(BF16), vb[:, vsl(h)], preferred_element_type=F32)
        if finish:
            o = o + of_ref[:, vsl(h)]
            o = o * lax.rsqrt(jnp.mean(o * o, axis=-1, keepdims=True) + GLA_NORM_EPS)
            o = o * ng_ref[...] * _silu(r_ref[:, vsl(h)])
        o_ref[:, vsl(h)] = o.astype(o_ref.dtype)

    if keep_state:
        @pl.when(c == pl.num_programs(2) - 1)
        def _():
            for h in heads:
                sf_ref[h] = st_ref[h].T


def _gla_scan(proj, low, wdir, bg, norm_g, init, o_first, states, bsz, length, dk, dv, direction, cc):
    reverse = direction == 1
    finish = o_first is not None
    keep_state = init is None
    nh = GLA_HEADS
    hps = GLA_HEADS_PER_STEP
    nc = length // cc
    t = bsz * length
    wk, wv = hps * dk, hps * dv
    kblk = (nh * dk) // wk
    vblk = (2 * nh * dk) // wv
    rblk = (2 * nh * dk + nh * dv) // wv
    assert nh % hps == 0 and (2 * nh * dk) % wv == 0
    lw = low.shape[1]

    def chunk(b, c):
        return b * nc + (nc - 1 - c if reverse else c)

    head = pl.BlockSpec((cc, wv), lambda b, h, c: (chunk(b, c), h))
    in_specs = [
        pl.BlockSpec((cc, wk), lambda b, h, c: (chunk(b, c), h)),
        pl.BlockSpec((cc, wk), lambda b, h, c: (chunk(b, c), kblk + h)),
        pl.BlockSpec((cc, wv), lambda b, h, c: (chunk(b, c), vblk + h)),
        pl.BlockSpec((cc, lw), lambda b, h, c: (chunk(b, c), 0)),
        pl.BlockSpec((None, lw, wk), lambda b, h, c: (direction, 0, h)),
        pl.BlockSpec((None, 1, wk), lambda b, h, c: (direction, 0, h)),
        pl.BlockSpec((cc, cc), lambda b, h, c: (0, 0)),
        pl.BlockSpec((cc, 3 * cc), lambda b, h, c: (0, 0)),
    ]
    tri = np.triu(np.ones((cc, cc), np.float32)) if reverse else np.tril(np.ones((cc, cc), np.float32))
    ins = [proj, proj, proj, low, wdir, bg, jnp.asarray(_gla_level_map(cc, reverse)),
           jnp.asarray(np.tile(tri, (1, 3)), dtype=BF16)]
    if init is not None:
        state, layer = init
        in_specs.append(pl.BlockSpec((None, None, None, hps, dk, dv),
                                     lambda b, h, c: (b, layer, direction, h, 0, 0)))
        ins.append(state)
    if finish:
        in_specs += [head, pl.BlockSpec((cc, wv), lambda b, h, c: (chunk(b, c), rblk + h)),
                     pl.BlockSpec((1, dv), lambda b, h, c: (0, 0))]
        ins += [o_first, proj, norm_g.reshape(1, dv)]
    out_specs = [head]
    out_shape = [jax.ShapeDtypeStruct((t, nh * dv), BF16 if finish else F32)]
    aliases = {}
    if keep_state:
        out_specs.append(pl.BlockSpec((None, None, hps, dk, dv), lambda b, h, c: (b, direction, h, 0, 0)))
        out_shape.append(jax.ShapeDtypeStruct((bsz, 2, nh, dk, dv), F32))
        if states is not None:
            in_specs.append(pl.BlockSpec(memory_space=pl.ANY))
            ins.append(states)
            aliases = {len(ins) - 1: 1}
    res = pl.pallas_call(
        functools.partial(_gla_scan_kernel, reverse=reverse, has_init=init is not None,
                          keep_state=keep_state, finish=finish, scale=dk ** -0.5, cc=cc, dk=dk, dv=dv),
        grid=(bsz, nh // hps, nc),
        in_specs=in_specs,
        out_specs=out_specs,
        out_shape=out_shape,
        input_output_aliases=aliases,
        scratch_shapes=[pltpu.VMEM((hps, dv, dk), F32), pltpu.VMEM((cc, wk), F32)],
        compiler_params=_params("parallel", "parallel", "arbitrary"),
    )(*ins)
    return (res[0], res[1]) if keep_state else (res[0], None)


def _gla_core(u, proj, bsz, length, init, gla):
    wg1, wg2, bg, norm_g = gla
    d = u.shape[1]
    rank = wg1.shape[2]
    dk_tot = wg2.shape[2]
    dk = dk_tot // GLA_HEADS
    dv = norm_g.shape[0]
    wlow = jnp.zeros((d, LANES), F32).at[:, :rank].set(wg1[0]).at[:, rank:2 * rank].set(wg1[1])
    low = _matmul(u, wlow.astype(BF16))
    wdir = jnp.zeros((2, LANES, dk_tot), F32).at[0, :rank].set(wg2[0]).at[1, rank:2 * rank].set(wg2[1])
    wdir = wdir.astype(BF16)
    cc = min(GLA_CHUNK, length)
    o, states = None, None
    for direction in (0, 1):
        o, states = _gla_scan(proj, low, wdir, bg.reshape(2, 1, dk_tot), norm_g, init, o, states,
                              bsz, length, dk, dv, direction, cc)
    return o, states


def kernel(x_prompt, x_sample, state_gla, c, c_ctx, ada_w, ada_b, ln_g, ln_b,
           hy_w_in, hy_conv_w, hy_conv_b, hy_fw_in, hy_fb_in, hy_fw_mid, hy_fb_mid,
           hy_freq, hy_fw_out, hy_fb_out, hy_skip, hy_w_out,
           sc_w_in, sc_conv_w, sc_w_out,
           gla_w_in, gla_wg1, gla_wg2, gla_bg, gla_norm_g, gla_w_out):
    depth, d, _ = ada_w.shape
    alpha = (2 * depth) ** 0.25
    bp, lp, _ = x_prompt.shape
    bs, ls, _ = x_sample.shape
    assert 1 + bs <= MOD_ROWS

    cond = jnp.zeros((MOD_ROWS, d), F32).at[0].set(c_ctx).at[1:1 + bs].set(c)
    mods = _modulation(cond, ada_w, ada_b)
    mods = mods.reshape(depth, MOD_ROWS, 3, 1, d).transpose(0, 2, 1, 3, 4)

    streams = [
        dict(x=x_prompt, b=bp, l=lp, period=lp, row0=0, stride=0, tt=min(TT, lp)),
        dict(x=x_sample, b=bs, l=ls, period=min(GRID_W, ls), row0=1, stride=1, tt=min(TT, ls)),
    ]
    for s in streams:
        s["u"] = _premod(s["x"], mods[0, 0], mods[0, 1], s["row0"], s["stride"], s["tt"])

    new_states = []
    for i in range(depth):
        kind = i % N_MIXERS
        j = i // N_MIXERS
        if kind == 0:
            w_in, w_out = hy_w_in[j].astype(BF16), hy_w_out[j].astype(BF16)
            hy = (hy_fw_in[j], hy_fb_in[j], hy_fw_mid[j], hy_fb_mid[j],
                  hy_freq[j], hy_fw_out[j], hy_fb_out[j], hy_skip[j])
        elif kind == 1:
            w_in, w_out = sc_w_in[j].astype(BF16), sc_w_out[j].astype(BF16)
        else:
            w_in, w_out = gla_w_in[j].astype(BF16), gla_w_out[j].astype(BF16)
            gla = (gla_wg1[j], gla_wg2[j], gla_bg[j], gla_norm_g[j])
        for si, s in enumerate(streams):
            u2 = s["u"].reshape(s["b"] * s["l"], d)
            if kind == 0:
                vprime, xg = _inproj_fused(u2, w_in, "hyena", hy_conv_w[j], hy_conv_b[j], s["period"])
                z = _hyena_core(vprime, xg, s["b"], s["l"], hy)
            elif kind == 1:
                z = _inproj_fused(u2, w_in, "sconv", sc_conv_w[j], None, s["period"])
            else:
                init = None if si == 0 else (state_gla, j)
                z, finals = _gla_core(u2, _matmul(u2, w_in), s["b"], s["l"], init, gla)
                if si == 0:
                    new_states.append(finals.astype(x_prompt.dtype))
            h = _matmul(z, w_out, BF16)
            nxt = None if i + 1 == depth else (mods[i + 1, 0], mods[i + 1, 1])
            s["x"], s["u"] = _ln_step(s["x"], h, mods[i, 2], ln_g[i], ln_b[i], nxt, alpha,
                                      s["row0"], s["stride"], s["tt"])
    return streams[0]["x"], streams[1]["x"], jnp.stack(new_states, axis=1)
```
